```python
import jax, jax.numpy as jnp
from jax import lax
import numpy as np

D_MODEL = 2048
BATCH = 4
SEQ = 4096
DEPTH = 4

RWKV_HEADS = 16
RWKV_HEAD_DIM = 64
RWKV_WIDTH = RWKV_HEADS * RWKV_HEAD_DIM
LORA_DECAY = 64
LORA_A = 64
LORA_VRES = 32
LORA_GATE = 160
RWKV_GN_EPS = 64e-5
RET_HEADS = 8
RET_HEAD_DIM = 128
RET_WIDTH = RET_HEADS * RET_HEAD_DIM
RET_CHUNK = 128
RET_GN_EPS = 1e-5
ROPE_BASE = 10000.0
N_EXPERTS = 32
TOP_K = 4
D_EXPERT = D_MODEL // 4
SWIGLU_ALPHA = 1.702
SWIGLU_LIMIT = 7.0
MOE_BLOCK = 128
NORM_EPS = 1e-6
N_ADA = 6
N_SHIFT = 3 * RWKV_WIDTH + LORA_DECAY + LORA_A + LORA_GATE
N_RET = 4 * RET_WIDTH
N_IN = N_SHIFT + N_RET + 2 * D_MODEL
SHIFT_SPLITS = (RWKV_WIDTH, 2 * RWKV_WIDTH, 3 * RWKV_WIDTH,
                3 * RWKV_WIDTH + LORA_DECAY, 3 * RWKV_WIDTH + LORA_DECAY + LORA_A)

kernel_name = "hybrid_rwkv7_retention_moe_adaln"


def rms_norm(x, g):
    xf = x.astype(jnp.float32)
    y = xf * lax.rsqrt(jnp.mean(xf * xf, axis=-1, keepdims=True) + NORM_EPS)
    return (y * g.astype(jnp.float32)).astype(x.dtype)


def head_group_norm(y, eps):
    mu = jnp.mean(y, axis=-1, keepdims=True)
    var = jnp.mean(jnp.square(y - mu), axis=-1, keepdims=True)
    return (y - mu) * lax.rsqrt(var + eps)


def token_shift(t):
    return jnp.pad(t[:, :-1], ((0, 0), (1, 0), (0, 0)))


def rotary(x, positions):
    half = x.shape[-1] // 2
    inv_freq = ROPE_BASE ** (-jnp.arange(half, dtype=jnp.float32) / half)
    ang = positions.astype(jnp.float32)[..., None] * inv_freq
    cos = jnp.cos(ang)[:, :, None, :]
    sin = jnp.sin(ang)[:, :, None, :]
    x1 = x[..., :half].astype(jnp.float32)
    x2 = x[..., half:].astype(jnp.float32)
    return jnp.concatenate([x1 * cos - x2 * sin, x1 * sin + x2 * cos], axis=-1).astype(x.dtype)


def rwkv7_recurrence(r, w, k, v, a, b):
    bsz, _, nh, n = r.shape
    xs = tuple(t.transpose(1, 0, 2, 3) for t in (r, w, k, v, a, b))

    def step(state, inp):
        r_t, w_t, k_t, v_t, a_t, b_t = inp
        sa = jnp.einsum('bhij,bhj->bhi', state, a_t)
        state = (state * w_t[:, :, None, :] + sa[..., None] * b_t[:, :, None, :]
                 + v_t[..., None] * k_t[:, :, None, :])
        return state, jnp.einsum('bhij,bhj->bhi', state, r_t)

    _, out = lax.scan(step, jnp.zeros((bsz, nh, n, n), jnp.float32), xs)
    return out.transpose(1, 0, 2, 3)


def retention_chunkwise(q, k, v):
    bsz, seq, nh, dk = q.shape
    dv = v.shape[-1]
    nc = seq // RET_CHUNK
    log_gamma = jnp.log1p(-jnp.exp2(-5.0 - jnp.arange(nh, dtype=jnp.float32)))
    qc = q.reshape(bsz, nc, RET_CHUNK, nh, dk)
    kc = k.reshape(bsz, nc, RET_CHUNK, nh, dk)
    vc = v.reshape(bsz, nc, RET_CHUNK, nh, dv)
    idx = jnp.arange(RET_CHUNK, dtype=jnp.float32)
    rel = idx[:, None] - idx[None, :]
    decay = jnp.where(rel >= 0, jnp.exp(jnp.maximum(rel, 0.0)[None] * log_gamma[:, None, None]), 0.0)
    scores = jnp.einsum('bnihd,bnjhd->bnhij', qc, kc) * decay[None, None]
    intra = jnp.einsum('bnhij,bnjhe->bnihe', scores, vc)
    zeta = jnp.exp((RET_CHUNK - 1 - idx)[:, None] * log_gamma[None, :])
    chunk_kv = jnp.einsum('bnjhd,bnjhe->nbhde', kc * zeta[None, None, :, :, None], vc)
    chunk_decay = jnp.exp(RET_CHUNK * log_gamma)

    def step(state, kv):
        return state * chunk_decay[None, :, None, None] + kv, state

    _, r_prev = lax.scan(step, jnp.zeros((bsz, nh, dk, dv), jnp.float32), chunk_kv)
    xi = jnp.exp((idx + 1.0)[:, None] * log_gamma[None, :])
    cross = jnp.einsum('bnihd,nbhde->bnihe', qc, r_prev) * xi[None, None, :, :, None]
    return (intra + cross).reshape(bsz, seq, nh, dv)


def hybrid_mixer(h, positions, v_first, w_in, shift_mu, w0, w2, a0, a2, g2, kk_scale, k_a, r_k,
                 lnx_g, lnx_b, ret_gn_g, w_up_a, w_up_b, w_out, vres):
    bsz, seq, _ = h.shape
    dt = h.dtype
    f32 = jnp.float32
    proj = h @ w_in
    p_shift = proj[..., :N_SHIFT]
    p_ret = proj[..., N_SHIFT:N_SHIFT + N_RET]
    p_gate = proj[..., N_SHIFT + N_RET:]

    p_shift = p_shift + (token_shift(p_shift) - p_shift) * shift_mu
    r, k, v, xw, xa, xg = jnp.split(p_shift, SHIFT_SPLITS, axis=-1)
    w_log = -jax.nn.softplus(-(w0 + jnp.tanh(xw) @ w2).astype(f32)) - 0.5
    decay = jnp.exp(-jnp.exp(w_log))
    a = jax.nn.sigmoid(a0 + xa @ a2)
    g = jax.nn.sigmoid(xg) @ g2
    if vres is None:
        v_first = v
    else:
        mu_v, v1, v2, v0 = vres
        xv = h + (token_shift(h) - h) * mu_v
        v = v + (v_first - v) * jax.nn.sigmoid(v0 + (xv @ v1) @ v2)

    def heads(t):
        return t.reshape(bsz, seq, RWKV_HEADS, RWKV_HEAD_DIM).astype(f32)

    kk = heads(k * kk_scale)
    kk = kk / jnp.maximum(jnp.sqrt(jnp.sum(kk * kk, axis=-1, keepdims=True)), 1e-12)
    k = k * (1 + (a - 1) * k_a)
    rh, kh, vh, ah = heads(r), heads(k), heads(v), heads(a)
    o = rwkv7_recurrence(rh, heads(decay), kh, vh, -kk, kk * ah)
    o = head_group_norm(o, RWKV_GN_EPS).reshape(bsz, seq, RWKV_WIDTH) * lnx_g + lnx_b
    bonus = (jnp.sum(rh * kh * r_k, axis=-1, keepdims=True) * vh).reshape(bsz, seq, RWKV_WIDTH)
    y_a = ((o + bonus) * g).astype(dt)

    q, kr, vr, gr = jnp.split(p_ret, 4, axis=-1)

    def rheads(t):
        return t.reshape(bsz, seq, RET_HEADS, RET_HEAD_DIM)

    q = rotary(rheads(q), positions)
    kr = rotary(rheads(kr), positions) * (RET_HEAD_DIM ** -0.5)
    o = retention_chunkwise(q.astype(f32), kr.astype(f32), rheads(vr).astype(f32))
    o = head_group_norm(o, RET_GN_EPS).reshape(bsz, seq, RET_WIDTH) * ret_gn_g
    y_b = (jax.nn.silu(gr) * o).astype(dt)

    g_a, g_b = jnp.split(p_gate, 2, axis=-1)
    y = jax.nn.sigmoid(g_a) * (y_a @ w_up_a) + jax.nn.sigmoid(g_b) * (y_b @ w_up_b)
    return y @ w_out, v_first


def moe_ffn(h, router_w, router_b, w_gate, b_gate, w_up, b_up, w_down, b_down):
    bsz, seq, d = h.shape
    n_tok = bsz * seq
    hf = h.reshape(n_tok, d)
    logits = (hf @ router_w + router_b).astype(jnp.float32)
    top_val, top_idx = lax.top_k(logits, TOP_K)
    gates = jax.nn.softmax(top_val, axis=-1)
    n_assign = n_tok * TOP_K
    flat_e = top_idx.reshape(n_assign)
    flat_tok = jnp.repeat(jnp.arange(n_tok, dtype=jnp.int32), TOP_K)
    order = jnp.argsort(flat_e, stable=True)
    se, st, sw = flat_e[order], flat_tok[order], gates.reshape(n_assign)[order]
    counts = jnp.bincount(flat_e, length=N_EXPERTS)
    starts = jnp.cumsum(counts) - counts
    pcounts = (counts + MOE_BLOCK - 1) // MOE_BLOCK * MOE_BLOCK
    pends = jnp.cumsum(pcounts)
    pstarts = pends - pcounts
    dest = pstarts[se] + (jnp.arange(n_assign, dtype=jnp.int32) - starts[se])
    n_pad = n_assign + N_EXPERTS * MOE_BLOCK
    n_blocks = n_pad // MOE_BLOCK
    pad_tok = jnp.zeros((n_pad,), jnp.int32).at[dest].set(st)
    pad_w = jnp.zeros((n_pad,), jnp.float32).at[dest].set(sw)
    blk_e = jnp.minimum(jnp.searchsorted(pends, jnp.arange(n_blocks, dtype=jnp.int32) * MOE_BLOCK,
                                         side='right'), N_EXPERTS - 1)

    def step(acc, blk):
        tok_b, w_b, e = blk
        xb = hf[tok_b]
        gt = jnp.minimum(xb @ w_gate[e] + b_gate[e], SWIGLU_LIMIT)
        up = jnp.clip(xb @ w_up[e] + b_up[e], -SWIGLU_LIMIT, SWIGLU_LIMIT)
        act = (up + 1) * (gt * jax.nn.sigmoid(gt * SWIGLU_ALPHA))
        yb = act @ w_down[e] + b_down[e]
        return acc.at[tok_b].add(yb * w_b[:, None].astype(yb.dtype)), None

    out, _ = lax.scan(step, jnp.zeros((n_tok, d), h.dtype),
                      (pad_tok.reshape(n_blocks, MOE_BLOCK), pad_w.reshape(n_blocks, MOE_BLOCK), blk_e))
    return out.reshape(bsz, seq, d)


def setup_inputs(seed: int = 0) -> dict:
    key = jax.random.key(seed)
    keys = jax.random.split(key, 40)
    cnt = iter(range(40))
    L, D, E, F = DEPTH, D_MODEL, N_EXPERTS, D_EXPERT
    f32 = jnp.float32

    def nrm(shape, scale):
        return jax.random.normal(keys[next(cnt)], shape, f32) * scale

    def uni(shape, lo, hi):
        return jax.random.uniform(keys[next(cnt)], shape, f32, lo, hi)

    x = nrm((BATCH, SEQ, D), 1.0)
    c = nrm((BATCH, D), 1.0)
    positions = (jax.random.randint(keys[next(cnt)], (BATCH, 1), 0, 1024, jnp.int32)
                 + jnp.arange(SEQ, dtype=jnp.int32)[None, :])
    return {
        "x": x, "c": c, "positions": positions,
        "ada_w": nrm((L, D, N_ADA * D), 0.5 * D ** -0.5),
        "ada_b": nrm((L, N_ADA * D), 0.01),
        "norm1_g": 1.0 + nrm((L, D), 0.02),
        "norm2_g": 1.0 + nrm((L, D), 0.02),
        "w_in": nrm((L, D, N_IN), D ** -0.5),
        "shift_mu": uni((L, N_SHIFT), 0.0, 1.0),
        "rwkv_w0": uni((L, RWKV_WIDTH), -6.5, -1.0),
        "rwkv_w2": nrm((L, LORA_DECAY, RWKV_WIDTH), 0.5 * LORA_DECAY ** -0.5),
        "rwkv_a0": nrm((L, RWKV_WIDTH), 0.1),
        "rwkv_a2": nrm((L, LORA_A, RWKV_WIDTH), 0.5 * LORA_A ** -0.5),
        "rwkv_g2": nrm((L, LORA_GATE, RWKV_WIDTH), LORA_GATE ** -0.5),
        "rwkv_kk_scale": 0.85 + nrm((L, RWKV_WIDTH), 0.02),
        "rwkv_k_a": 1.0 + nrm((L, RWKV_WIDTH), 0.02),
        "rwkv_r_k": -0.04 + nrm((L, RWKV_HEADS, RWKV_HEAD_DIM), 0.02),
        "rwkv_lnx_g": 1.0 + nrm((L, RWKV_WIDTH), 0.02),
        "rwkv_lnx_b": nrm((L, RWKV_WIDTH), 0.02),
        "vres_mu": uni((L - 1, D), 0.0, 1.0),
        "vres_w1": nrm((L - 1, D, LORA_VRES), D ** -0.5),
        "vres_w2": nrm((L - 1, LORA_VRES, RWKV_WIDTH), 0.5 * LORA_VRES ** -0.5),
        "vres_v0": 1.0 + nrm((L - 1, RWKV_WIDTH), 0.1),
        "ret_gn_g": 1.0 + nrm((L, RET_WIDTH), 0.02),
        "w_up_a": nrm((L, RWKV_WIDTH, D), RWKV_WIDTH ** -0.5),
        "w_up_b": nrm((L, RET_WIDTH, D), RET_WIDTH ** -0.5),
        "w_out": nrm((L, D, D), D ** -0.5),
        "router_w": nrm((L, D, E), D ** -0.5),
        "router_b": nrm((L, E), 0.01),
        "exp_w_gate": nrm((L, E, D, F), D ** -0.5),
        "exp_b_gate": nrm((L, E, F), 0.01),
        "exp_w_up": nrm((L, E, D, F), D ** -0.5),
        "exp_b_up": nrm((L, E, F), 0.01),
        "exp_w_down": nrm((L, E, F, D), F ** -0.5),
        "exp_b_down": nrm((L, E, D), 0.01),
        "final_g": 1.0 + nrm((D,), 0.02),
    }


def reference(x, c, positions, ada_w, ada_b, norm1_g, norm2_g, w_in, shift_mu, rwkv_w0, rwkv_w2,
              rwkv_a0, rwkv_a2, rwkv_g2, rwkv_kk_scale, rwkv_k_a, rwkv_r_k, rwkv_lnx_g, rwkv_lnx_b,
              vres_mu, vres_w1, vres_w2, vres_v0, ret_gn_g, w_up_a, w_up_b, w_out, router_w, router_b,
              exp_w_gate, exp_b_gate, exp_w_up, exp_b_up, exp_w_down, exp_b_down, final_g):
    v_first = None
    cond = jax.nn.silu(c)
    for l in range(DEPTH):
        mod = cond @ ada_w[l] + ada_b[l]
        sh1, sc1, gt1, sh2, sc2, gt2 = [m[:, None, :] for m in jnp.split(mod, N_ADA, axis=-1)]
        h = rms_norm(x, norm1_g[l]) * (1 + sc1) + sh1
        vres = None if l == 0 else (vres_mu[l - 1], vres_w1[l - 1], vres_w2[l - 1], vres_v0[l - 1])
        y, v_first = hybrid_mixer(h, positions, v_first, w_in[l], shift_mu[l], rwkv_w0[l], rwkv_w2[l],
                                  rwkv_a0[l], rwkv_a2[l], rwkv_g2[l], rwkv_kk_scale[l], rwkv_k_a[l],
                                  rwkv_r_k[l], rwkv_lnx_g[l], rwkv_lnx_b[l], ret_gn_g[l],
                                  w_up_a[l], w_up_b[l], w_out[l], vres)
        x = x + gt1 * y
        h = rms_norm(x, norm2_g[l]) * (1 + sc2) + sh2
        x = x + gt2 * moe_ffn(h, router_w[l], router_b[l], exp_w_gate[l], exp_b_gate[l],
                              exp_w_up[l], exp_b_up[l], exp_w_down[l], exp_b_down[l])
    return rms_norm(x, final_g)
```

```python
import functools

import jax
import jax.numpy as jnp
from jax import lax
from jax.experimental import pallas as pl
from jax.experimental.pallas import tpu as pltpu

F32 = jnp.float32
BF16 = jnp.bfloat16
I32 = jnp.int32

D_MODEL = 2048
RWKV_HEAD_DIM = 64
RWKV_WIDTH = 1024
LORA_DECAY = 64
LORA_A = 64
LORA_VRES = 32
LORA_GATE = 160
RWKV_GN_EPS = 64e-5
RET_HEADS = 8
RET_HEAD_DIM = 128
RET_WIDTH = 1024
RET_GN_EPS = 1e-5
ROPE_BASE = 10000.0
N_EXPERTS = 32
TOP_K = 4
D_EXPERT = 512
SWIGLU_ALPHA = 1.702
SWIGLU_LIMIT = 7.0
NORM_EPS = 1e-6
N_ADA = 6
N_SHIFT = 3 * RWKV_WIDTH + LORA_DECAY + LORA_A + LORA_GATE
N_RET = 4 * RET_WIDTH

LANES = 128
SUBLANES = 8

RET_OFF = 0
GATE_OFF = RET_OFF + N_RET
RW_OFF = GATE_OFF + 2 * D_MODEL
XG_OFF = RW_OFF + 3 * RWKV_WIDTH
XG_W = 256
XWA_OFF = XG_OFF + XG_W
XV_OFF = XWA_OFF + LANES
N_PROJ = XV_OFF + LANES

RWKV_CHUNK = 64
RWKV_GROUP = 2
RWKV_ROWS = 512
RET_ROWS = 256
MOE_ROWS = 256
COMB_ROWS = 128
VMEM_LIMIT = 56 * 1024 * 1024


def _dot(a, b):
    return jnp.dot(a, b, preferred_element_type=F32)


def _dot_nt(a, b):
    return lax.dot_general(a, b, (((1,), (1,)), ((), ())), preferred_element_type=F32)


def _split_bf16(x):
    hi = x.astype(BF16)
    lo = (x - hi.astype(F32)).astype(BF16)
    return hi, lo


def _sigmoid(x):
    return 1.0 / (1.0 + jnp.exp(-x))


def _cparams(sem):
    return pltpu.CompilerParams(dimension_semantics=sem, vmem_limit_bytes=VMEM_LIMIT)


def _mod_kernel(c_ref, w_ref, b_ref, o_ref):
    c = c_ref[...]
    cs = c * _sigmoid(c)
    o_ref[0] = _dot(cs.astype(BF16), w_ref[0].astype(BF16)) + b_ref[0]


def _modulation(c_pad, ada_w, ada_b):
    n_layers, d, n = ada_w.shape
    tn = 1024
    return pl.pallas_call(
        _mod_kernel,
        grid=(n_layers, n // tn),
        in_specs=[
            pl.BlockSpec((SUBLANES, d), lambda l, j: (0, 0)),
            pl.BlockSpec((1, d, tn), lambda l, j: (l, 0, j)),
            pl.BlockSpec((1, 1, tn), lambda l, j: (l, 0, j)),
        ],
        out_specs=pl.BlockSpec((1, SUBLANES, tn), lambda l, j: (l, 0, j)),
        out_shape=jax.ShapeDtypeStruct((n_layers, SUBLANES, n), F32),
        compiler_params=_cparams(("parallel", "parallel")),
        name="adaln_mod",
    )(c_pad, ada_w, ada_b.reshape(n_layers, 1, n))


def _trig_kernel(pos_ref, invf_ref, cos_ref, sin_ref):
    ang = pos_ref[...].astype(F32) * invf_ref[...]
    lane = lax.broadcasted_iota(I32, ang.shape, 1)
    s = jnp.sin(ang)
    cos_ref[...] = jnp.cos(ang)
    sin_ref[...] = jnp.where(lane < RET_HEAD_DIM // 2, -s, s)


def _rotary_tables(pos_col, invf2):
    t = pos_col.shape[0]
    tm = 1024
    return pl.pallas_call(
        _trig_kernel,
        grid=(t // tm,),
        in_specs=[pl.BlockSpec((tm, 1), lambda i: (i, 0)), pl.BlockSpec((1, LANES), lambda i: (0, 0))],
        out_specs=[pl.BlockSpec((tm, LANES), lambda i: (i, 0))] * 2,
        out_shape=[jax.ShapeDtypeStruct((t, LANES), F32)] * 2,
        compiler_params=_cparams(("parallel",)),
        name="rotary_tables",
    )(pos_col, invf2)


def _in_kernel(x_ref, g_ref, sc_ref, sh_ref, w_ref, o_ref, h_scr):
    @pl.when(pl.program_id(1) == 0)
    def _():
        x = x_ref[...]
        y = x * lax.rsqrt(jnp.mean(x * x, axis=-1, keepdims=True) + NORM_EPS)
        h_scr[...] = (y * g_ref[...] * (1.0 + sc_ref[0]) + sh_ref[0]).astype(BF16)

    o_ref[...] = _dot(h_scr[...], w_ref[...])


def _in_proj(x, g, sc, sh, w, seq):
    t, d = x.shape
    n = w.shape[1]
    tm = min(1024, seq)
    tn = 512
    return pl.pallas_call(
        _in_kernel,
        grid=(t // tm, n // tn),
        in_specs=[
            pl.BlockSpec((tm, d), lambda i, j: (i, 0)),
            pl.BlockSpec((1, d), lambda i, j: (0, 0)),
            pl.BlockSpec((1, 1, d), lambda i, j: (i * tm // seq, 0, 0)),
            pl.BlockSpec((1, 1, d), lambda i, j: (i * tm // seq, 0, 0)),
            pl.BlockSpec((d, tn), lambda i, j: (0, j)),
        ],
        out_specs=pl.BlockSpec((tm, tn), lambda i, j: (i, j)),
        out_shape=jax.ShapeDtypeStruct((t, n), F32),
        scratch_shapes=[pltpu.VMEM((tm, d), BF16)],
        compiler_params=_cparams(("parallel", "arbitrary")),
        name="norm_in_proj",
    )(x, g, sc, sh, w)


def _rwkv_kernel(*refs, rows, lw, has_vres):
    c = RWKV_CHUNK
    n = RWKV_HEAD_DIM
    g_heads = lw // n
    it = iter(refs)
    r_ref, k_ref, v_ref, xg_ref, xwa_ref, xv_ref = (next(it) for _ in range(6))
    mu_r, mu_k, mu_v, mu_xg, mu_xwa, mu_xv = (next(it) for _ in range(6))
    w0, a0, kks, ka, rk, lng, lnb, v0 = (next(it) for _ in range(8))
    w2_ref, a2_ref, g2_ref, v2_ref = (next(it) for _ in range(4))
    vf_ref = next(it) if has_vres else None
    ya_ref = next(it)
    vfo_ref = None if has_vres else next(it)
    c_r, c_k, c_v, c_xg, c_xwa, c_xv = (next(it) for _ in range(6))
    st_ref = next(it)
    at_s, rt_s, bt_s, kt_s, bv_s, km_s, cs_s, vv_s, o_s = (next(it) for _ in range(9))

    @pl.when(pl.program_id(2) == 0)
    def _():
        for cr in (c_r, c_k, c_v, c_xg, c_xwa, c_xv):
            cr[...] = jnp.zeros_like(cr)
        st_ref[...] = jnp.zeros_like(st_ref)

    def shift(p_ref, mu_ref, carry_ref):
        p = p_ref[...]
        row = lax.broadcasted_iota(I32, p.shape, 0)
        prev = jnp.where(row == 0, carry_ref[0:1, :], pltpu.roll(p, 1, 0))
        carry_ref[0:1, :] = p[rows - 1:rows, :]
        return p + (prev - p) * mu_ref[...]

    r = shift(r_ref, mu_r, c_r)
    k = shift(k_ref, mu_k, c_k)
    v = shift(v_ref, mu_v, c_v)
    xg = shift(xg_ref, mu_xg, c_xg)
    xwa = shift(xwa_ref, mu_xwa, c_xwa)

    z = w0[...] + _dot(jnp.tanh(xwa).astype(BF16), w2_ref[...])
    softplus_neg = jnp.maximum(-z, 0.0) + jnp.log1p(jnp.exp(-jnp.abs(z)))
    lw_log = -jnp.exp(-softplus_neg - 0.5)
    a = _sigmoid(a0[...] + _dot(xwa.astype(BF16), a2_ref[...]))
    gate = _dot(_sigmoid(xg).astype(BF16), g2_ref[...])
    if has_vres:
        xv = shift(xv_ref, mu_xv, c_xv)
        v = v + (vf_ref[...] - v) * _sigmoid(v0[...] + _dot(xv.astype(BF16), v2_ref[...]))
    else:
        vfo_ref[...] = v

    hrow = lax.broadcasted_iota(I32, (lw, lw), 0) // n
    hcol = lax.broadcasted_iota(I32, (lw, lw), 1) // n
    head_ones = (hrow == hcol).astype(BF16)

    def headsum(x):
        hi, lo = _split_bf16(x)
        return _dot(hi, head_ones) + _dot(lo, head_ones)

    kk0 = k * kks[...]
    kk = kk0 / jnp.maximum(jnp.sqrt(headsum(kk0 * kk0)), 1e-12)
    km = k * (1.0 + (a - 1.0) * ka[...])
    bonus = headsum(r * km * rk[...]) * v

    trow = lax.broadcasted_iota(I32, (rows, rows), 0)
    tcol = lax.broadcasted_iota(I32, (rows, rows), 1)
    tri = ((trow // c == tcol // c) & (tcol <= trow)).astype(BF16)
    lhi, llo = _split_bf16(lw_log)
    cs = _dot(tri, lhi) + _dot(tri, llo)
    b_vec = kk * a
    w_inv = jnp.exp(-cs)
    at_s[...] = -kk * jnp.exp(cs - lw_log)
    rt_s[...] = r * jnp.exp(cs)
    bt_s[...] = b_vec * w_inv
    kt_s[...] = km * w_inv
    bv_s[...] = b_vec
    km_s[...] = km
    cs_s[...] = cs
    vv_s[...] = v

    def chunk_body(ci, carry):
        rs = pl.ds(pl.multiple_of(ci * c, c), c)
        row = lax.broadcasted_iota(I32, (lw, lw), 0)
        col = lax.broadcasted_iota(I32, (lw, lw), 1)
        same_head = (row // n) == (col // n)

        def stack(x):
            return jnp.where(same_head, jnp.concatenate([x] * g_heads, axis=0), 0.0)

        at2 = stack(at_s[rs, :])
        rt2 = stack(rt_s[rs, :])
        bt2 = stack(bt_s[rs, :])
        kt2 = stack(kt_s[rs, :])
        v2 = stack(vv_s[rs, :]).astype(BF16)
        cs_c = cs_s[rs, :]
        cs_last = cs_c[c - 1:c, :]
        tail = jnp.exp(cs_last - cs_c)
        bh2 = stack(bv_s[rs, :] * tail)
        kh2 = stack(km_s[rs, :] * tail)

        lhs = jnp.concatenate([at2, rt2], axis=0).astype(BF16)
        rhs = jnp.concatenate([bt2, kt2], axis=0).astype(BF16)
        aa = _dot_nt(lhs, rhs)
        strict = row > col
        lower = row >= col
        a_ab = jnp.where(strict, aa[:lw, :lw], 0.0)
        a_ak = jnp.where(strict, aa[:lw, lw:], 0.0)
        a_rb = jnp.where(lower, aa[lw:, :lw], 0.0)
        a_rk = jnp.where(lower, aa[lw:, lw:], 0.0)

        x = jnp.where(row == col, 1.0, 0.0) + jnp.where(((row ^ col) == 1) & ((row & 1) == 1), a_ab, 0.0)
        b = 2
        while b < c:
            sel = ((row // (2 * b)) == (col // (2 * b))) & ((row & b) != 0) & ((col & b) == 0)
            xb = x.astype(BF16)
            y = _dot(xb, jnp.where(sel, a_ab, 0.0).astype(BF16))
            x = x + _dot(y.astype(BF16), xb)
            b *= 2

        akv = _dot(a_ak.astype(BF16), v2)
        p12 = _dot(x.astype(BF16), jnp.concatenate([at2, akv], axis=1).astype(BF16))
        p12b = p12.astype(BF16)
        arb = _dot(a_rb.astype(BF16), p12b)
        qp = rt2 + arb[:, :lw]
        o_in = arb[:, lw:] + _dot(a_rk.astype(BF16), v2)
        st = st_ref[...]
        stb = st.astype(BF16)
        o2 = _dot(qp.astype(BF16), stb) + o_in
        o = o2[0:c, :]
        for h in range(1, g_heads):
            o = o + o2[h * c:(h + 1) * c, :]
        o_s[rs, :] = o

        bp = _dot(bh2.T.astype(BF16), p12b)
        ht = bp[:, lw:] + _dot(kh2.T.astype(BF16), v2)
        wc_col = jnp.sum(jnp.where(row == col, jnp.exp(cs_last), 0.0), axis=1, keepdims=True)
        st_ref[...] = wc_col * st + _dot(bp[:, :lw].astype(BF16), stb) + ht
        return carry

    lax.fori_loop(0, rows // c, chunk_body, 0)

    o = o_s[...]
    mean = headsum(o) * (1.0 / n)
    dev = o - mean
    var = headsum(dev * dev) * (1.0 / n)
    on = dev * lax.rsqrt(var + RWKV_GN_EPS) * lng[...] + lnb[...]
    ya_ref[...] = ((on + bonus) * gate).astype(ya_ref.dtype)


def _rwkv_mix(proj, mu_ext, pw, v_first, batch, seq):
    t = proj.shape[0]
    rows = min(RWKV_ROWS, seq)
    lw = RWKV_GROUP * RWKV_HEAD_DIM
    nj = seq // rows
    ng = RWKV_WIDTH // lw
    has_vres = v_first is not None

    def rowblk(b, g, j):
        return b * nj + j

    def colspec(width, off, per_group):
        if per_group:
            return pl.BlockSpec((rows, width), lambda b, g, j: (rowblk(b, g, j), off // width + g))
        return pl.BlockSpec((rows, width), lambda b, g, j: (rowblk(b, g, j), off // width))

    def muspec(width, off, per_group):
        if per_group:
            return pl.BlockSpec((1, width), lambda b, g, j: (0, off // width + g))
        return pl.BlockSpec((1, width), lambda b, g, j: (0, off // width))

    layout = [(lw, RW_OFF, True), (lw, RW_OFF + RWKV_WIDTH, True), (lw, RW_OFF + 2 * RWKV_WIDTH, True),
              (XG_W, XG_OFF, False), (LANES, XWA_OFF, False), (LANES, XV_OFF, False)]
    in_specs = [colspec(*a) for a in layout] + [muspec(*a) for a in layout]
    args = [proj] * 6 + [mu_ext] * 6
    pspec = pl.BlockSpec((1, lw), lambda b, g, j: (0, g))
    for name in ("w0", "a0", "kks", "ka", "rk", "lng", "lnb", "v0"):
        in_specs.append(pspec)
        args.append(pw[name])
    for name, kdim in (("w2", LANES), ("a2", LANES), ("g2", XG_W), ("v2", LANES)):
        in_specs.append(pl.BlockSpec((kdim, lw), lambda b, g, j: (0, g)))
        args.append(pw[name])
    io_spec = pl.BlockSpec((rows, lw), lambda b, g, j: (rowblk(b, g, j), g))
    out_specs = [io_spec]
    out_shape = [jax.ShapeDtypeStruct((t, RWKV_WIDTH), BF16)]
    if has_vres:
        in_specs.append(io_spec)
        args.append(v_first)
    else:
        out_specs.append(io_spec)
        out_shape.append(jax.ShapeDtypeStruct((t, RWKV_WIDTH), F32))
    scratch = [pltpu.VMEM((SUBLANES, lw), F32)] * 3 + [pltpu.VMEM((SUBLANES, XG_W), F32)]
    scratch += [pltpu.VMEM((SUBLANES, LANES), F32)] * 2
    scratch += [pltpu.VMEM((lw, lw), F32)]
    scratch += [pltpu.VMEM((rows, lw), F32)] * 9
    outs = pl.pallas_call(
        functools.partial(_rwkv_kernel, rows=rows, lw=lw, has_vres=has_vres),
        grid=(batch, ng, nj),
        in_specs=in_specs,
        out_specs=out_specs,
        out_shape=out_shape,
        scratch_shapes=scratch,
        compiler_params=_cparams(("parallel", "parallel", "arbitrary")),
        name="rwkv7_mix",
    )(*args)
    if has_vres:
        return outs[0], v_first
    return outs[0], outs[1]


def _ret_kernel(q_ref, k_ref, v_ref, g_ref, cos_ref, sin_ref, lg_ref, gn_ref, o_ref, st_ref, *, rows):
    @pl.when(pl.program_id(2) == 0)
    def _():
        st_ref[...] = jnp.zeros_like(st_ref)

    lg = lg_ref[0]
    cos = cos_ref[...]
    sin = sin_ref[...]

    def rot(x):
        return x * cos + pltpu.roll(x, RET_HEAD_DIM // 2, 1) * sin

    q = rot(q_ref[...])
    k = rot(k_ref[...]) * (RET_HEAD_DIM ** -0.5)
    v = v_ref[...].astype(BF16)
    row = lax.broadcasted_iota(I32, (rows, rows), 0)
    col = lax.broadcasted_iota(I32, (rows, rows), 1)
    rel = (row - col).astype(F32)
    lg_sq = jnp.concatenate([lg] * (rows // LANES), axis=1) if rows > LANES else lg
    decay = jnp.where(rel >= 0, jnp.exp(jnp.maximum(rel, 0.0) * lg_sq), 0.0)
    qb = q.astype(BF16)
    scores = _dot_nt(qb, k.astype(BF16)) * decay
    intra = _dot(scores.astype(BF16), v)
    idx = lax.broadcasted_iota(I32, (rows, LANES), 0).astype(F32)
    xi = jnp.exp((idx + 1.0) * lg)
    st = st_ref[...]
    cross = _dot(qb, st.astype(BF16)) * xi
    zeta = jnp.exp((rows - 1.0 - idx) * lg)
    kz = (k * zeta).T.astype(BF16)
    chunk_decay = jnp.exp(rows * lg)
    st_ref[...] = st * chunk_decay + _dot(kz, v)
    o = intra + cross
    mean = jnp.mean(o, axis=-1, keepdims=True)
    dev = o - mean
    var = jnp.mean(dev * dev, axis=-1, keepdims=True)
    on = dev * lax.rsqrt(var + RET_GN_EPS) * gn_ref[...]
    gr = g_ref[...]
    o_ref[...] = (gr * _sigmoid(gr) * on).astype(o_ref.dtype)


def _retention(proj, cos2, sin2, log_gamma, gn_g, batch, seq):
    t = proj.shape[0]
    rows = min(RET_ROWS, seq)
    nj = seq // rows
    hd = RET_HEAD_DIM
    nh = RET_HEADS

    def pspec(off):
        return pl.BlockSpec((rows, hd), lambda b, h, j: (b * nj + j, off // hd + h))

    tspec = pl.BlockSpec((rows, LANES), lambda b, h, j: (b * nj + j, 0))
    return pl.pallas_call(
        functools.partial(_ret_kernel, rows=rows),
        grid=(batch, nh, nj),
        in_specs=[pspec(RET_OFF), pspec(RET_OFF + RET_WIDTH), pspec(RET_OFF + 2 * RET_WIDTH),
                  pspec(RET_OFF + 3 * RET_WIDTH), tspec, tspec,
                  pl.BlockSpec((1, 1, LANES), lambda b, h, j: (h, 0, 0)),
                  pl.BlockSpec((1, hd), lambda b, h, j: (0, h))],
        out_specs=pl.BlockSpec((rows, hd), lambda b, h, j: (b * nj + j, h)),
        out_shape=jax.ShapeDtypeStruct((t, RET_WIDTH), BF16),
        scratch_shapes=[pltpu.VMEM((hd, hd), F32)],
        compiler_params=_cparams(("parallel", "parallel", "arbitrary")),
        name="retention",
    )(proj, proj, proj, proj, cos2, sin2, log_gamma, gn_g)


def _out_kernel(ya_ref, yb_ref, ga_ref, gb_ref, x_ref, gt1_ref, n2g_ref, sc2_ref, sh2_ref,
                wa_ref, wb_ref, wo_ref, rwh_ref, rwl_ref, rb_ref, xo_ref, h2_ref, lg_ref):
    ua = _dot(ya_ref[...], wa_ref[...])
    ub = _dot(yb_ref[...], wb_ref[...])
    y = _sigmoid(ga_ref[...]) * ua + _sigmoid(gb_ref[...]) * ub
    xn = x_ref[...] + gt1_ref[0] * _dot(y.astype(BF16), wo_ref[...])
    xo_ref[...] = xn
    yn = xn * lax.rsqrt(jnp.mean(xn * xn, axis=-1, keepdims=True) + NORM_EPS)
    h2 = yn * n2g_ref[...] * (1.0 + sc2_ref[0]) + sh2_ref[0]
    h2_ref[...] = h2
    hi, lo = _split_bf16(h2)
    lg_ref[...] = _dot(hi, rwh_ref[...]) + _dot(hi, rwl_ref[...]) + _dot(lo, rwh_ref[...]) + rb_ref[...]


def _merge_out(ya, yb, proj, x, gt1, n2g, sc2, sh2, wa, wb, wo, rwh, rwl, rb, seq):
    t, d = x.shape
    tm = 256
    nb = seq // tm
    const = lambda i: (0, 0)
    mod = pl.BlockSpec((1, 1, d), lambda i: (i // nb, 0, 0))
    single = pl.Buffered(1)
    return pl.pallas_call(
        _out_kernel,
        grid=(t // tm,),
        in_specs=[
            pl.BlockSpec((tm, RWKV_WIDTH), lambda i: (i, 0)),
            pl.BlockSpec((tm, RET_WIDTH), lambda i: (i, 0)),
            pl.BlockSpec((tm, d), lambda i: (i, GATE_OFF // d)),
            pl.BlockSpec((tm, d), lambda i: (i, GATE_OFF // d + 1)),
            pl.BlockSpec((tm, d), lambda i: (i, 0)),
            mod, pl.BlockSpec((1, d), const), mod, mod,
            pl.BlockSpec((RWKV_WIDTH, d), const, pipeline_mode=single),
            pl.BlockSpec((RET_WIDTH, d), const, pipeline_mode=single),
            pl.BlockSpec((d, d), const, pipeline_mode=single),
            pl.BlockSpec((d, LANES), const, pipeline_mode=single),
            pl.BlockSpec((d, LANES), const, pipeline_mode=single),
            pl.BlockSpec((1, LANES), const),
        ],
        out_specs=[pl.BlockSpec((tm, d), lambda i: (i, 0)), pl.BlockSpec((tm, d), lambda i: (i, 0)),
                   pl.BlockSpec((tm, LANES), lambda i: (i, 0))],
        out_shape=[jax.ShapeDtypeStruct((t, d), F32), jax.ShapeDtypeStruct((t, d), F32),
                   jax.ShapeDtypeStruct((t, LANES), F32)],
        compiler_params=_cparams(("parallel",)),
        name="merge_out_norm_router",
    )(ya, yb, proj, proj, x, gt1, n2g, sc2, sh2, wa, wb, wo, rwh, rwl, rb)


def _route_kernel(lg_ref, meta_ref, gate_ref, cnt_ref, carry_ref, *, tm):
    @pl.when(pl.program_id(0) == 0)
    def _():
        carry_ref[...] = jnp.zeros_like(carry_ref)

    l = lg_ref[...]
    lane = lax.broadcasted_iota(I32, l.shape, 1)
    lane_f = lane.astype(F32)
    vals, idxs = [], []
    for _ in range(TOP_K):
        m = jnp.max(l, axis=-1, keepdims=True)
        idx = jnp.min(jnp.where(l == m, lane_f, float(LANES)), axis=-1, keepdims=True)
        vals.append(m)
        idxs.append(idx)
        l = jnp.where(lane_f == idx, -jnp.inf, l)
    es = [jnp.exp(vv - vals[0]) for vv in vals]
    denom = es[0] + es[1] + es[2] + es[3]
    sel = jnp.zeros(l.shape, F32)
    for idx in idxs:
        sel = sel + jnp.where(lane_f == idx, 1.0, 0.0)
    row = lax.broadcasted_iota(I32, (tm, tm), 0)
    col = lax.broadcasted_iota(I32, (tm, tm), 1)
    tri = (col <= row).astype(BF16)
    cum = _dot(tri, sel.astype(BF16)) + carry_ref[0:1, :]
    excl = cum - sel
    carry_ref[0:1, :] = cum[tm - 1:tm, :]
    meta = jnp.zeros(l.shape, F32)
    gates = jnp.zeros(l.shape, F32)
    for kk in range(TOP_K):
        rank = jnp.sum(jnp.where(lane_f == idxs[kk], excl, 0.0), axis=-1, keepdims=True)
        meta = meta + jnp.where(lane == kk, idxs[kk], 0.0) + jnp.where(lane == TOP_K + kk, rank, 0.0)
        gates = gates + jnp.where(lane == kk, es[kk] / denom, 0.0)
    meta_ref[...] = meta.astype(I32)
    gate_ref[...] = gates
    cnt_ref[...] = jnp.broadcast_to(cum[tm - 1:tm, :], cnt_ref.shape).astype(I32)


def _route(logits):
    t = logits.shape[0]
    tm = 256
    return pl.pallas_call(
        functools.partial(_route_kernel, tm=tm),
        grid=(t // tm,),
        in_specs=[pl.BlockSpec((tm, LANES), lambda i: (i, 0))],
        out_specs=[pl.BlockSpec((tm, LANES), lambda i: (i, 0)), pl.BlockSpec((tm, LANES), lambda i: (i, 0)),
                   pl.BlockSpec((SUBLANES, LANES), lambda i: (0, 0))],
        out_shape=[jax.ShapeDtypeStruct((t, LANES), I32), jax.ShapeDtypeStruct((t, LANES), F32),
                   jax.ShapeDtypeStruct((SUBLANES, LANES), I32)],
        scratch_shapes=[pltpu.VMEM((SUBLANES, LANES), F32)],
        compiler_params=_cparams(("arbitrary",)),
        name="router_topk",
    )(logits)


def _moe_kernel(be_ref, nu_ref, tok_ref, h_hbm, wgu_ref, bgu_ref, wd_ref, bd_ref, ys_ref, xbuf, sem):
    del be_ref
    i = pl.program_id(0)
    n_used = nu_ref[0]
    blk = MOE_ROWS

    def row_copy(tok, r, slot):
        return pltpu.make_async_copy(h_hbm.at[pl.ds(tok, 1), :], xbuf.at[slot, pl.ds(r, 1), :], sem.at[slot])

    def issue(b, slot):
        def body(r, carry):
            row_copy(tok_ref[b * blk + r], r, slot).start()
            return carry
        lax.fori_loop(0, blk, body, 0)

    @pl.when(i == 0)
    def _():
        issue(0, 0)

    @pl.when(i + 1 < n_used)
    def _():
        issue(i + 1, (i + 1) % 2)

    @pl.when(i < n_used)
    def _():
        slot = i % 2
        pltpu.make_async_copy(h_hbm.at[pl.ds(0, blk), :], xbuf.at[slot], sem.at[slot]).wait()
        x = xbuf[slot].astype(BF16)
        gu = _dot(x, wgu_ref[0]) + bgu_ref[0]
        gt = jnp.minimum(gu[:, :D_EXPERT], SWIGLU_LIMIT)
        up = jnp.clip(gu[:, D_EXPERT:], -SWIGLU_LIMIT, SWIGLU_LIMIT)
        act = (up + 1.0) * (gt * _sigmoid(gt * SWIGLU_ALPHA))
        ys_ref[...] = _dot(act.astype(BF16), wd_ref[0]) + bd_ref[0]

    @pl.when(i >= n_used)
    def _():
        ys_ref[...] = jnp.zeros_like(ys_ref)


def _moe_experts(blk_e, n_used, tok_tab, h2, wgu, bgu, wd, bd):
    d = h2.shape[1]
    n_pad = tok_tab.shape[0]
    blk = MOE_ROWS
    nblk = n_pad // blk
    grid_spec = pltpu.PrefetchScalarGridSpec(
        num_scalar_prefetch=3,
        grid=(nblk,),
        in_specs=[
            pl.BlockSpec(memory_space=pl.ANY),
            pl.BlockSpec((1, d, 2 * D_EXPERT), lambda i, be, nu, tk: (be[i], 0, 0)),
            pl.BlockSpec((1, 1, 2 * D_EXPERT), lambda i, be, nu, tk: (be[i], 0, 0)),
            pl.BlockSpec((1, D_EXPERT, d), lambda i, be, nu, tk: (be[i], 0, 0)),
            pl.BlockSpec((1, 1, d), lambda i, be, nu, tk: (be[i], 0, 0)),
        ],
        out_specs=pl.BlockSpec((blk, d), lambda i, be, nu, tk: (i, 0)),
        scratch_shapes=[pltpu.VMEM((2, blk, d), F32), pltpu.SemaphoreType.DMA((2,))],
    )
    return pl.pallas_call(
        _moe_kernel,
        grid_spec=grid_spec,
        out_shape=jax.ShapeDtypeStruct((n_pad, d), F32),
        compiler_params=_cparams(("arbitrary",)),
        name="moe_experts",
    )(blk_e, n_used, tok_tab, h2, wgu, bgu, wd, bd)


def _comb_kernel(dest_ref, ys_hbm, x_ref, gate_ref, gt2_ref, fg_ref, o_ref, buf, sem, *, final, n_tiles):
    i = pl.program_id(0)
    tc = COMB_ROWS

    def issue(tile, slot):
        def body(r, carry):
            base = (tile * tc + r) * TOP_K
            for kk in range(TOP_K):
                pltpu.make_async_copy(ys_hbm.at[pl.ds(dest_ref[base + kk], 1), :],
                                      buf.at[slot, kk, pl.ds(r, 1), :], sem.at[slot]).start()
            return carry
        lax.fori_loop(0, tc, body, 0)

    @pl.when(i == 0)
    def _():
        issue(0, 0)

    @pl.when(i + 1 < n_tiles)
    def _():
        issue(i + 1, (i + 1) % 2)

    slot = i % 2
    for kk in range(TOP_K):
        pltpu.make_async_copy(ys_hbm.at[pl.ds(0, tc), :], buf.at[slot, kk], sem.at[slot]).wait()
    gates = gate_ref[...]
    moe = gates[:, 0:1] * buf[slot, 0]
    for kk in range(1, TOP_K):
        moe = moe + gates[:, kk:kk + 1] * buf[slot, kk]
    xn = x_ref[...] + gt2_ref[0] * moe
    if final:
        xn = xn * lax.rsqrt(jnp.mean(xn * xn, axis=-1, keepdims=True) + NORM_EPS) * fg_ref[...]
    o_ref[...] = xn


def _moe_combine(dest_flat, ys, x, gates, gt2, final_g, seq, final):
    t, d = x.shape
    tc = COMB_ROWS
    nb = seq // tc
    n_tiles = t // tc
    grid_spec = pltpu.PrefetchScalarGridSpec(
        num_scalar_prefetch=1,
        grid=(n_tiles,),
        in_specs=[
            pl.BlockSpec(memory_space=pl.ANY),
            pl.BlockSpec((tc, d), lambda i, ds: (i, 0)),
            pl.BlockSpec((tc, LANES), lambda i, ds: (i, 0)),
            pl.BlockSpec((1, 1, d), lambda i, ds: (i // nb, 0, 0)),
            pl.BlockSpec((1, d), lambda i, ds: (0, 0)),
        ],
        out_specs=pl.BlockSpec((tc, d), lambda i, ds: (i, 0)),
        scratch_shapes=[pltpu.VMEM((2, TOP_K, tc, d), F32), pltpu.SemaphoreType.DMA((2,))],
    )
    return pl.pallas_call(
        functools.partial(_comb_kernel, final=final, n_tiles=n_tiles),
        grid_spec=grid_spec,
        out_shape=jax.ShapeDtypeStruct((t, d), F32),
        compiler_params=_cparams(("arbitrary",)),
        name="moe_combine",
    )(dest_flat, ys, x, gates, gt2, final_g)


def _pad_rows(w, n_rows, at=0):
    out = jnp.zeros((n_rows, w.shape[1]), w.dtype)
    return out.at[at:at + w.shape[0]].set(w)


def _layer_params(l, p):
    d = D_MODEL
    w_in = p["w_in"][l]
    sh, ret, gate = w_in[:, :N_SHIFT], w_in[:, N_SHIFT:N_SHIFT + N_RET], w_in[:, N_SHIFT + N_RET:]
    rkv = sh[:, :3 * RWKV_WIDTH]
    xw = sh[:, 3 * RWKV_WIDTH:3 * RWKV_WIDTH + LORA_DECAY]
    xa = sh[:, 3 * RWKV_WIDTH + LORA_DECAY:3 * RWKV_WIDTH + LORA_DECAY + LORA_A]
    xg = sh[:, 3 * RWKV_WIDTH + LORA_DECAY + LORA_A:]
    xg = jnp.pad(xg, ((0, 0), (0, XG_W - LORA_GATE)))
    mu = p["shift_mu"][l]
    mu_xg = jnp.pad(mu[3 * RWKV_WIDTH + LORA_DECAY + LORA_A:], (0, XG_W - LORA_GATE))
    mu_xv = jnp.concatenate([jnp.zeros((LORA_VRES,), F32), jnp.ones((LORA_VRES,), F32),
                             jnp.zeros((LANES - 2 * LORA_VRES,), F32)])
    if l == 0:
        xv = jnp.zeros((d, LANES), F32)
        v2 = jnp.zeros((LANES, RWKV_WIDTH), F32)
        v0 = jnp.zeros((RWKV_WIDTH,), F32)
    else:
        mu_v = p["vres_mu"][l - 1][:, None]
        v1 = p["vres_w1"][l - 1]
        xv = jnp.concatenate([v1 * (1.0 - mu_v), v1 * mu_v, jnp.zeros((d, LANES - 2 * LORA_VRES), F32)], axis=1)
        w2v = p["vres_w2"][l - 1]
        v2 = _pad_rows(jnp.concatenate([w2v, w2v], axis=0), LANES)
        v0 = p["vres_v0"][l - 1]
    w_proj = jnp.concatenate([ret, gate, rkv, xg, xw, xa, xv], axis=1).astype(BF16)
    mu_ext = jnp.concatenate([jnp.zeros((N_RET + 2 * d,), F32), mu[:3 * RWKV_WIDTH], mu_xg,
                              mu[3 * RWKV_WIDTH:3 * RWKV_WIDTH + LORA_DECAY + LORA_A], mu_xv])[None, :]
    row = lambda a: a.reshape(1, -1)
    pw = {
        "w0": row(p["rwkv_w0"][l]), "a0": row(p["rwkv_a0"][l]), "kks": row(p["rwkv_kk_scale"][l]),
        "ka": row(p["rwkv_k_a"][l]), "rk": row(p["rwkv_r_k"][l]), "lng": row(p["rwkv_lnx_g"][l]),
        "lnb": row(p["rwkv_lnx_b"][l]), "v0": row(v0),
        "w2": _pad_rows(p["rwkv_w2"][l], LANES, 0).astype(BF16),
        "a2": _pad_rows(p["rwkv_a2"][l], LANES, LORA_DECAY).astype(BF16),
        "g2": _pad_rows(p["rwkv_g2"][l], XG_W, 0).astype(BF16),
        "v2": v2.astype(BF16),
    }
    rw = jnp.pad(p["router_w"][l], ((0, 0), (0, LANES - N_EXPERTS)))
    rwh, rwl = _split_bf16(rw)
    rb = jnp.concatenate([p["router_b"][l], jnp.full((LANES - N_EXPERTS,), -1e30, F32)])[None, :]
    wgu = jnp.concatenate([p["exp_w_gate"][l], p["exp_w_up"][l]], axis=-1).astype(BF16)
    bgu = jnp.concatenate([p["exp_b_gate"][l], p["exp_b_up"][l]], axis=-1)[:, None, :]
    return {
        "w_proj": w_proj, "mu_ext": mu_ext, "pw": pw,
        "wa": p["w_up_a"][l].astype(BF16), "wb": p["w_up_b"][l].astype(BF16), "wo": p["w_out"][l].astype(BF16),
        "rwh": rwh, "rwl": rwl, "rb": rb,
        "wgu": wgu, "bgu": bgu, "wd": p["exp_w_down"][l].astype(BF16), "bd": p["exp_b_down"][l][:, None, :],
        "n1g": row(p["norm1_g"][l]), "n2g": row(p["norm2_g"][l]), "gn_g": row(p["ret_gn_g"][l]),
    }


def _routing_tables(meta, counts, n_tok):
    blk = MOE_ROWS
    idx = meta[:, :TOP_K]
    rank = meta[:, TOP_K:2 * TOP_K]
    cnt = counts[0, :N_EXPERTS]
    pcnt = (cnt + blk - 1) // blk * blk
    pends = jnp.cumsum(pcnt)
    pstarts = pends - pcnt
    dest = (pstarts[idx] + rank).astype(I32)
    n_pad = n_tok * TOP_K + N_EXPERTS * blk
    nblk = n_pad // blk
    tok = jnp.broadcast_to(jnp.arange(n_tok, dtype=I32)[:, None], (n_tok, TOP_K))
    tok_tab = jnp.zeros((n_pad,), I32).at[dest.reshape(-1)].set(tok.reshape(-1))
    blk_e = jnp.minimum(jnp.searchsorted(pends, jnp.arange(nblk, dtype=I32) * blk, side="right"),
                        N_EXPERTS - 1).astype(I32)
    n_used = (pends[-1] // blk).astype(I32).reshape(1)
    return dest.reshape(-1), tok_tab, blk_e, n_used


def kernel(x, c, positions, ada_w, ada_b, norm1_g, norm2_g, w_in, shift_mu, rwkv_w0, rwkv_w2, rwkv_a0, rwkv_a2, rwkv_g2, rwkv_kk_scale, rwkv_k_a, rwkv_r_k, rwkv_lnx_g, rwkv_lnx_b, vres_mu, vres_w1, vres_w2, vres_v0, ret_gn_g, w_up_a, w_up_b, w_out, router_w, router_b, exp_w_gate, exp_b_gate, exp_w_up, exp_b_up, exp_w_down, exp_b_down, final_g):
    p = dict(w_in=w_in, shift_mu=shift_mu, rwkv_w0=rwkv_w0, rwkv_w2=rwkv_w2, rwkv_a0=rwkv_a0,
             rwkv_a2=rwkv_a2, rwkv_g2=rwkv_g2, rwkv_kk_scale=rwkv_kk_scale, rwkv_k_a=rwkv_k_a,
             rwkv_r_k=rwkv_r_k, rwkv_lnx_g=rwkv_lnx_g, rwkv_lnx_b=rwkv_lnx_b, vres_mu=vres_mu,
             vres_w1=vres_w1, vres_w2=vres_w2, vres_v0=vres_v0, ret_gn_g=ret_gn_g, w_up_a=w_up_a,
             w_up_b=w_up_b, w_out=w_out, router_w=router_w, router_b=router_b, exp_w_gate=exp_w_gate,
             exp_b_gate=exp_b_gate, exp_w_up=exp_w_up, exp_b_up=exp_b_up, exp_w_down=exp_w_down,
             exp_b_down=exp_b_down, norm1_g=norm1_g, norm2_g=norm2_g)
    batch, seq, d = x.shape
    n_layers = ada_w.shape[0]
    n_tok = batch * seq
    assert batch <= SUBLANES and d == D_MODEL

    c_pad = jnp.zeros((SUBLANES, d), F32).at[:batch].set(c)
    mod = _modulation(c_pad, ada_w, ada_b)[:, :batch].reshape(n_layers, batch, N_ADA, 1, d)

    half = RET_HEAD_DIM // 2
    inv_freq = ROPE_BASE ** (-jnp.arange(half, dtype=F32) / half)
    invf2 = jnp.concatenate([inv_freq, inv_freq])[None, :]
    cos2, sin2 = _rotary_tables(positions.reshape(n_tok, 1), invf2)
    log_gamma = jnp.log1p(-jnp.exp2(-5.0 - jnp.arange(RET_HEADS, dtype=F32)))
    log_gamma = jnp.broadcast_to(log_gamma[:, None, None], (RET_HEADS, 1, LANES))

    xf = x.reshape(n_tok, d)
    v_first = None
    for l in range(n_layers):
        lp = _layer_params(l, p)
        sh1, sc1, gt1, sh2, sc2, gt2 = (mod[l, :, i] for i in range(N_ADA))
        proj = _in_proj(xf, lp["n1g"], sc1, sh1, lp["w_proj"], seq)
        ya, v_first = _rwkv_mix(proj, lp["mu_ext"], lp["pw"], v_first, batch, seq)
        yb = _retention(proj, cos2, sin2, log_gamma, lp["gn_g"], batch, seq)
        xf, h2, logits = _merge_out(ya, yb, proj, xf, gt1, lp["n2g"], sc2, sh2, lp["wa"], lp["wb"],
                                    lp["wo"], lp["rwh"], lp["rwl"], lp["rb"], seq)
        meta, gates, counts = _route(logits)
        dest, tok_tab, blk_e, n_used = _routing_tables(meta, counts, n_tok)
        ys = _moe_experts(blk_e, n_used, tok_tab, h2, lp["wgu"], lp["bgu"], lp["wd"], lp["bd"])
        xf = _moe_combine(dest, ys, xf, gates, gt2, final_g.reshape(1, d), seq, l == n_layers - 1)
    return xf.reshape(batch, seq, d)
```

```python
import functools
import math

import jax
import jax.numpy as jnp
from jax import lax
from jax.experimental import pallas as pl
from jax.experimental.pallas import tpu as pltpu

F32 = jnp.float32
BF16 = jnp.bfloat16
I32 = jnp.int32

D_MODEL = 2048
RWKV_HEAD_DIM = 64
RWKV_WIDTH = 1024
LORA_DECAY = 64
LORA_A = 64
LORA_VRES = 32
LORA_GATE = 160
RWKV_GN_EPS = 64e-5
RET_HEADS = 8
RET_HEAD_DIM = 128
RET_WIDTH = 1024
RET_GN_EPS = 1e-5
ROPE_BASE = 10000.0
N_EXPERTS = 32
TOP_K = 4
D_EXPERT = 512
SWIGLU_ALPHA = 1.702
SWIGLU_LIMIT = 7.0
NORM_EPS = 1e-6
N_ADA = 6
N_SHIFT = 3 * RWKV_WIDTH + LORA_DECAY + LORA_A + LORA_GATE
N_RET = 4 * RET_WIDTH

LANES = 128
SUBLANES = 8
MXU_DIM = 256

RET_OFF = 0
GATE_OFF = RET_OFF + N_RET
RW_OFF = GATE_OFF + 2 * D_MODEL
XG_OFF = RW_OFF + 3 * RWKV_WIDTH
XG_W = 256
XWA_OFF = XG_OFF + XG_W
XV_OFF = XWA_OFF + LANES
N_PROJ = XV_OFF + LANES

RWKV_CHUNK = 64
RWKV_ROWS = 256
RWKV_GROUPS_PER_BODY = 4
RET_ROWS = 256
MOE_ROWS = 256
COMB_ROWS = 128
VMEM_LIMIT = 56 * 1024 * 1024


def _dot(a, b):
    return jnp.dot(a, b, preferred_element_type=F32)


def _dot_nt(a, b):
    return lax.dot_general(a, b, (((1,), (1,)), ((), ())), preferred_element_type=F32)


def _split_bf16(x):
    hi = x.astype(BF16)
    lo = (x - hi.astype(F32)).astype(BF16)
    return hi, lo


def _sigmoid(x):
    return 1.0 / (1.0 + jnp.exp(-x))


def _cparams(sem):
    return pltpu.CompilerParams(dimension_semantics=sem, vmem_limit_bytes=VMEM_LIMIT)


def _mod_kernel(c_ref, w_ref, b_ref, o_ref):
    c = c_ref[...]
    cs = c * _sigmoid(c)
    o_ref[0] = _dot(cs.astype(BF16), w_ref[0].astype(BF16)) + b_ref[0]


def _modulation(c_pad, ada_w, ada_b):
    n_layers, d, n = ada_w.shape
    tn = 1024
    return pl.pallas_call(
        _mod_kernel,
        grid=(n_layers, n // tn),
        in_specs=[
            pl.BlockSpec((SUBLANES, d), lambda l, j: (0, 0)),
            pl.BlockSpec((1, d, tn), lambda l, j: (l, 0, j)),
            pl.BlockSpec((1, 1, tn), lambda l, j: (l, 0, j)),
        ],
        out_specs=pl.BlockSpec((1, SUBLANES, tn), lambda l, j: (l, 0, j)),
        out_shape=jax.ShapeDtypeStruct((n_layers, SUBLANES, n), F32),
        compiler_params=_cparams(("parallel", "parallel")),
        name="adaln_mod",
    )(c_pad, ada_w, ada_b.reshape(n_layers, 1, n))


def _trig_kernel(pos_ref, invf_ref, cos_ref, sin_ref):
    ang = pos_ref[...].astype(F32) * invf_ref[...]
    lane = lax.broadcasted_iota(I32, ang.shape, 1)
    s = jnp.sin(ang)
    cos_ref[...] = jnp.cos(ang)
    sin_ref[...] = jnp.where(lane < RET_HEAD_DIM // 2, -s, s)


def _rotary_tables(pos_col, invf2):
    t = pos_col.shape[0]
    tm = 1024
    return pl.pallas_call(
        _trig_kernel,
        grid=(t // tm,),
        in_specs=[pl.BlockSpec((tm, 1), lambda i: (i, 0)), pl.BlockSpec((1, LANES), lambda i: (0, 0))],
        out_specs=[pl.BlockSpec((tm, LANES), lambda i: (i, 0))] * 2,
        out_shape=[jax.ShapeDtypeStruct((t, LANES), F32)] * 2,
        compiler_params=_cparams(("parallel",)),
        name="rotary_tables",
    )(pos_col, invf2)


def _in_kernel(x_ref, g_ref, sc_ref, sh_ref, w_ref, o_ref, h_scr):
    @pl.when(pl.program_id(1) == 0)
    def _():
        x = x_ref[...]
        y = x * lax.rsqrt(jnp.mean(x * x, axis=-1, keepdims=True) + NORM_EPS)
        h_scr[...] = (y * g_ref[...] * (1.0 + sc_ref[0]) + sh_ref[0]).astype(BF16)

    o_ref[...] = _dot(h_scr[...], w_ref[...])


def _in_proj(x, g, sc, sh, w, seq):
    t, d = x.shape
    n = w.shape[1]
    tm = min(1024, seq)
    tn = 512
    return pl.pallas_call(
        _in_kernel,
        grid=(t // tm, n // tn),
        in_specs=[
            pl.BlockSpec((tm, d), lambda i, j: (i, 0)),
            pl.BlockSpec((1, d), lambda i, j: (0, 0)),
            pl.BlockSpec((1, 1, d), lambda i, j: (i * tm // seq, 0, 0)),
            pl.BlockSpec((1, 1, d), lambda i, j: (i * tm // seq, 0, 0)),
            pl.BlockSpec((d, tn), lambda i, j: (0, j)),
        ],
        out_specs=pl.BlockSpec((tm, tn), lambda i, j: (i, j)),
        out_shape=jax.ShapeDtypeStruct((t, n), F32),
        scratch_shapes=[pltpu.VMEM((tm, d), BF16)],
        compiler_params=_cparams(("parallel", "arbitrary")),
        name="norm_in_proj",
    )(x, g, sc, sh, w)


def _bdot(a, b):
    return jnp.einsum("bij,bjk->bik", a, b, preferred_element_type=F32)


def _bdot_nt(a, b):
    return jnp.einsum("bik,bjk->bij", a, b, preferred_element_type=F32)


def _rwkv_chunk_terms(at, rt, bt, kt, bh, kh, v, wc_row):
    c = RWKV_CHUNK
    n = RWKV_HEAD_DIM
    lw = LANES
    nb = at.shape[0]
    reps = lw // c
    row = lax.broadcasted_iota(I32, (nb, lw, lw), 1)
    col = lax.broadcasted_iota(I32, (nb, lw, lw), 2)
    same_head = (row // n) == (col // n)

    def stack(x):
        return jnp.where(same_head, jnp.concatenate([x] * reps, axis=1), 0.0)

    def transpose(x):
        return jnp.stack([x[i].T for i in range(nb)])

    at2 = stack(at)
    rt2 = stack(rt)
    v2 = stack(v).astype(BF16)
    lhs = jnp.concatenate([at2, rt2], axis=1).astype(BF16)
    rhs = jnp.concatenate([stack(bt), stack(kt)], axis=1).astype(BF16)
    aa = _bdot_nt(lhs, rhs)
    strict = row > col
    lower = row >= col
    a_ab = jnp.where(strict, aa[:, :lw, :lw], 0.0)
    a_ak = jnp.where(strict, aa[:, :lw, lw:], 0.0)
    a_rb = jnp.where(lower, aa[:, lw:, :lw], 0.0)
    a_rk = jnp.where(lower, aa[:, lw:, lw:], 0.0)

    x = jnp.where(row == col, 1.0, 0.0) + jnp.where(((row ^ col) == 1) & ((row & 1) == 1), a_ab, 0.0)
    b = 2
    while b < c:
        sel = ((row // (2 * b)) == (col // (2 * b))) & ((row & b) != 0) & ((col & b) == 0)
        xb = x.astype(BF16)
        y = _bdot(xb, jnp.where(sel, a_ab, 0.0).astype(BF16))
        x = x + _bdot(y.astype(BF16), xb)
        b *= 2

    akv = _bdot(a_ak.astype(BF16), v2)
    p12b = _bdot(x.astype(BF16), jnp.concatenate([at2, akv], axis=2).astype(BF16)).astype(BF16)
    arb = _bdot(a_rb.astype(BF16), p12b)
    qp = (rt2 + arb[:, :, :lw]).astype(BF16)
    o_in = arb[:, :, lw:] + _bdot(a_rk.astype(BF16), v2)
    bp = _bdot(transpose(stack(bh)).astype(BF16), p12b)
    ht = bp[:, :, lw:] + _bdot(transpose(stack(kh)).astype(BF16), v2)
    wc_col = jnp.sum(jnp.where(row == col, wc_row, 0.0), axis=2, keepdims=True)
    return qp, o_in, bp[:, :, :lw].astype(BF16), ht, wc_col


def _rwkv_kernel(*refs, rows, has_vres):
    c = RWKV_CHUNK
    n = RWKV_HEAD_DIM
    lw = LANES
    ng = RWKV_WIDTH // lw
    nc = rows // c
    gpb = RWKV_GROUPS_PER_BODY
    it = iter(refs)
    r_ref, k_ref, v_ref, xg_ref, xwa_ref, xv_ref = (next(it) for _ in range(6))
    mu_r, mu_k, mu_v, mu_xg, mu_xwa, mu_xv = (next(it) for _ in range(6))
    w0, a0, kks, ka, rk, lng, lnb, v0 = (next(it) for _ in range(8))
    w2_ref, a2_ref, g2_ref, v2_ref = (next(it) for _ in range(4))
    vf_ref = next(it) if has_vres else None
    ya_ref = next(it)
    vfo_ref = None if has_vres else next(it)
    c_r, c_k, c_v, c_xg, c_xwa, c_xv = (next(it) for _ in range(6))
    st_s = next(it)
    at_s, rt_s, bt_s, kt_s, bh_s, kh_s, vv_s, wc_s, o_s = (next(it) for _ in range(9))
    bonus_s, gate_s = next(it), next(it)

    @pl.when(pl.program_id(1) == 0)
    def _():
        for cr in (c_r, c_k, c_v, c_xg, c_xwa, c_xv):
            cr[...] = jnp.zeros_like(cr)
        st_s[...] = jnp.zeros_like(st_s)

    def shift(p_ref, mu_ref, carry_ref):
        p = p_ref[...]
        row = lax.broadcasted_iota(I32, p.shape, 0)
        prev = jnp.where(row == 0, carry_ref[0:1, :], pltpu.roll(p, 1, 0))
        carry_ref[0:1, :] = p[rows - 1:rows, :]
        return p + (prev - p) * mu_ref[...]

    r = shift(r_ref, mu_r, c_r)
    k = shift(k_ref, mu_k, c_k)
    v = shift(v_ref, mu_v, c_v)
    xg = shift(xg_ref, mu_xg, c_xg)
    xwa = shift(xwa_ref, mu_xwa, c_xwa)

    z = w0[...] + _dot(jnp.tanh(xwa).astype(BF16), w2_ref[...])
    lw_log = -math.exp(-0.5) * _sigmoid(z)
    a = _sigmoid(a0[...] + _dot(xwa.astype(BF16), a2_ref[...]))
    gate_s[...] = _dot(_sigmoid(xg).astype(BF16), g2_ref[...])
    if has_vres:
        xv = shift(xv_ref, mu_xv, c_xv)
        v = v + (vf_ref[...] - v) * _sigmoid(v0[...] + _dot(xv.astype(BF16), v2_ref[...]))
    else:
        vfo_ref[...] = v

    hrow = lax.broadcasted_iota(I32, (MXU_DIM, MXU_DIM), 0) // n
    hcol = lax.broadcasted_iota(I32, (MXU_DIM, MXU_DIM), 1) // n
    head_ones = (hrow == hcol).astype(BF16)

    def headsum(x):
        xb = x.astype(BF16)
        return jnp.concatenate([_dot(xb[:, i * MXU_DIM:(i + 1) * MXU_DIM], head_ones)
                                for i in range(RWKV_WIDTH // MXU_DIM)], axis=1)

    kk0 = k * kks[...]
    kk = kk0 * lax.rsqrt(jnp.maximum(headsum(kk0 * kk0), 1e-24))
    km = k * (1.0 + (a - 1.0) * ka[...])
    bonus_s[...] = headsum(r * km * rk[...]) * v

    trow = lax.broadcasted_iota(I32, (rows, rows), 0) // c
    tcol = lax.broadcasted_iota(I32, (rows, rows), 1) // c
    same_chunk = trow == tcol
    tri = (same_chunk & (lax.broadcasted_iota(I32, (rows, rows), 1)
                         <= lax.broadcasted_iota(I32, (rows, rows), 0))).astype(BF16)
    blk_ones = same_chunk.astype(BF16)
    lhi, llo = _split_bf16(lw_log)
    cs = _dot(tri, lhi) + _dot(tri, llo)
    ctot = _dot(blk_ones, lhi) + _dot(blk_ones, llo)
    b_vec = kk * a
    w_inv = jnp.exp(-cs)
    tail = jnp.exp(ctot - cs)
    planes = ((at_s, -kk * jnp.exp(cs - lw_log)), (rt_s, r * jnp.exp(cs)), (bt_s, b_vec * w_inv),
              (kt_s, km * w_inv), (bh_s, b_vec * tail), (kh_s, km * tail), (vv_s, v), (wc_s, jnp.exp(ctot)))
    for dst, val in planes:
        for g in range(ng):
            dst[g] = val[:, g * lw:(g + 1) * lw]

    def pair_body(gp, carry):
        gs = pl.ds(gp * gpb, gpb)

        def chunks(ref):
            return ref[gs].reshape(gpb * nc, c, lw)

        qp_b, o_in_b, bpg_b, ht_b, wc_col_b = _rwkv_chunk_terms(
            chunks(at_s), chunks(rt_s), chunks(bt_s), chunks(kt_s), chunks(bh_s), chunks(kh_s),
            chunks(vv_s), chunks(wc_s)[:, 0:1, :])
        states = [st_s[gp * gpb + gg] for gg in range(gpb)]
        for ci in range(nc):
            for gg in range(gpb):
                bi = gg * nc + ci
                qp, o_in, bpg, ht, wc_col = qp_b[bi], o_in_b[bi], bpg_b[bi], ht_b[bi], wc_col_b[bi]
                st = states[gg]
                stb = st.astype(BF16)
                o2 = _dot(qp, stb) + o_in
                o = o2[0:c, :]
                for h in range(1, lw // c):
                    o = o + o2[h * c:(h + 1) * c, :]
                o_s[gp * gpb + gg, pl.ds(ci * c, c), :] = o
                states[gg] = wc_col * st + _dot(bpg, stb) + ht
        for gg in range(gpb):
            st_s[gp * gpb + gg] = states[gg]
        return carry

    lax.fori_loop(0, ng // gpb, pair_body, 0)

    o = jnp.concatenate([o_s[g] for g in range(ng)], axis=1)
    mean = headsum(o) * (1.0 / n)
    dev = o - mean
    var = headsum(dev * dev) * (1.0 / n)
    on = dev * lax.rsqrt(var + RWKV_GN_EPS) * lng[...] + lnb[...]
    ya_ref[...] = ((on + bonus_s[...]) * gate_s[...]).astype(ya_ref.dtype)


def _rwkv_mix(proj, mu_ext, pw, v_first, batch, seq):
    t = proj.shape[0]
    rows = min(RWKV_ROWS, seq)
    w = RWKV_WIDTH
    nj = seq // rows
    ng = w // LANES
    has_vres = v_first is not None

    def colspec(width, off):
        return pl.BlockSpec((rows, width), lambda b, j: (b * nj + j, off // width))

    def muspec(width, off):
        return pl.BlockSpec((1, width), lambda b, j: (0, off // width))

    layout = [(w, RW_OFF), (w, RW_OFF + w), (w, RW_OFF + 2 * w), (XG_W, XG_OFF), (LANES, XWA_OFF), (LANES, XV_OFF)]
    in_specs = [colspec(*a) for a in layout] + [muspec(*a) for a in layout]
    args = [proj] * 6 + [mu_ext] * 6
    for name in ("w0", "a0", "kks", "ka", "rk", "lng", "lnb", "v0"):
        in_specs.append(pl.BlockSpec((1, w), lambda b, j: (0, 0)))
        args.append(pw[name])
    for name, kdim in (("w2", LANES), ("a2", LANES), ("g2", XG_W), ("v2", LANES)):
        in_specs.append(pl.BlockSpec((kdim, w), lambda b, j: (0, 0)))
        args.append(pw[name])
    io_spec = pl.BlockSpec((rows, w), lambda b, j: (b * nj + j, 0))
    out_specs = [io_spec]
    out_shape = [jax.ShapeDtypeStruct((t, w), BF16)]
    if has_vres:
        in_specs.append(io_spec)
        args.append(v_first)
    else:
        out_specs.append(io_spec)
        out_shape.append(jax.ShapeDtypeStruct((t, w), F32))
    scratch = [pltpu.VMEM((SUBLANES, w), F32)] * 3 + [pltpu.VMEM((SUBLANES, XG_W), F32)]
    scratch += [pltpu.VMEM((SUBLANES, LANES), F32)] * 2
    scratch += [pltpu.VMEM((ng, LANES, LANES), F32)]
    scratch += [pltpu.VMEM((ng, rows, LANES), F32)] * 9
    scratch += [pltpu.VMEM((rows, w), F32)] * 2
    outs = pl.pallas_call(
        functools.partial(_rwkv_kernel, rows=rows, has_vres=has_vres),
        grid=(batch, nj),
        in_specs=in_specs,
        out_specs=out_specs,
        out_shape=out_shape,
        scratch_shapes=scratch,
        compiler_params=_cparams(("parallel", "arbitrary")),
        name="rwkv7_mix",
    )(*args)
    if has_vres:
        return outs[0], v_first
    return outs[0], outs[1]


def _ret_kernel(q_ref, k_ref, v_ref, g_ref, cos_ref, sin_ref, lg_ref, gn_ref, o_ref, st_ref, *, rows):
    @pl.when(pl.program_id(2) == 0)
    def _():
        st_ref[...] = jnp.zeros_like(st_ref)

    lg = lg_ref[0]
    cos = cos_ref[...]
    sin = sin_ref[...]

    def rot(x):
        return x * cos + pltpu.roll(x, RET_HEAD_DIM // 2, 1) * sin

    q = rot(q_ref[...])
    k = rot(k_ref[...]) * (RET_HEAD_DIM ** -0.5)
    v = v_ref[...].astype(BF16)
    row = lax.broadcasted_iota(I32, (rows, rows), 0)
    col = lax.broadcasted_iota(I32, (rows, rows), 1)
    rel = (row - col).astype(F32)
    lg_sq = jnp.concatenate([lg] * (rows // LANES), axis=1) if rows > LANES else lg
    decay = jnp.where(rel >= 0, jnp.exp(jnp.maximum(rel, 0.0) * lg_sq), 0.0)
    qb = q.astype(BF16)
    scores = _dot_nt(qb, k.astype(BF16)) * decay
    intra = _dot(scores.astype(BF16), v)
    idx = lax.broadcasted_iota(I32, (rows, LANES), 0).astype(F32)
    xi = jnp.exp((idx + 1.0) * lg)
    st = st_ref[...]
    cross = _dot(qb, st.astype(BF16)) * xi
    zeta = jnp.exp((rows - 1.0 - idx) * lg)
    kz = (k * zeta).T.astype(BF16)
    chunk_decay = jnp.exp(rows * lg)
    st_ref[...] = st * chunk_decay + _dot(kz, v)
    o = intra + cross
    mean = jnp.mean(o, axis=-1, keepdims=True)
    dev = o - mean
    var = jnp.mean(dev * dev, axis=-1, keepdims=True)
    on = dev * lax.rsqrt(var + RET_GN_EPS) * gn_ref[...]
    gr = g_ref[...]
    o_ref[...] = (gr * _sigmoid(gr) * on).astype(o_ref.dtype)


def _retention(proj, cos2, sin2, log_gamma, gn_g, batch, seq):
    t = proj.shape[0]
    rows = min(RET_ROWS, seq)
    nj = seq // rows
    hd = RET_HEAD_DIM
    nh = RET_HEADS

    def pspec(off):
        return pl.BlockSpec((rows, hd), lambda b, h, j: (b * nj + j, off // hd + h))

    tspec = pl.BlockSpec((rows, LANES), lambda b, h, j: (b * nj + j, 0))
    return pl.pallas_call(
        functools.partial(_ret_kernel, rows=rows),
        grid=(batch, nh, nj),
        in_specs=[pspec(RET_OFF), pspec(RET_OFF + RET_WIDTH), pspec(RET_OFF + 2 * RET_WIDTH),
                  pspec(RET_OFF + 3 * RET_WIDTH), tspec, tspec,
                  pl.BlockSpec((1, 1, LANES), lambda b, h, j: (h, 0, 0)),
                  pl.BlockSpec((1, hd), lambda b, h, j: (0, h))],
        out_specs=pl.BlockSpec((rows, hd), lambda b, h, j: (b * nj + j, h)),
        out_shape=jax.ShapeDtypeStruct((t, RET_WIDTH), BF16),
        scratch_shapes=[pltpu.VMEM((hd, hd), F32)],
        compiler_params=_cparams(("parallel", "parallel", "arbitrary")),
        name="retention",
    )(proj, proj, proj, proj, cos2, sin2, log_gamma, gn_g)


def _out_kernel(ya_ref, yb_ref, ga_ref, gb_ref, x_ref, gt1_ref, n2g_ref, sc2_ref, sh2_ref,
                wa_ref, wb_ref, wo_ref, rwh_ref, rwl_ref, rb_ref, xo_ref, h2_ref, lg_ref):
    ua = _dot(ya_ref[...], wa_ref[...])
    ub = _dot(yb_ref[...], wb_ref[...])
    y = _sigmoid(ga_ref[...]) * ua + _sigmoid(gb_ref[...]) * ub
    xn = x_ref[...] + gt1_ref[0] * _dot(y.astype(BF16), wo_ref[...])
    xo_ref[...] = xn
    yn = xn * lax.rsqrt(jnp.mean(xn * xn, axis=-1, keepdims=True) + NORM_EPS)
    h2 = yn * n2g_ref[...] * (1.0 + sc2_ref[0]) + sh2_ref[0]
    h2_ref[...] = h2
    hi, lo = _split_bf16(h2)
    lg_ref[...] = _dot(hi, rwh_ref[...]) + _dot(hi, rwl_ref[...]) + _dot(lo, rwh_ref[...]) + rb_ref[...]


def _merge_out(ya, yb, proj, x, gt1, n2g, sc2, sh2, wa, wb, wo, rwh, rwl, rb, seq):
    t, d = x.shape
    tm = 256
    nb = seq // tm
    const = lambda i: (0, 0)
    mod = pl.BlockSpec((1, 1, d), lambda i: (i // nb, 0, 0))
    single = pl.Buffered(1)
    return pl.pallas_call(
        _out_kernel,
        grid=(t // tm,),
        in_specs=[
            pl.BlockSpec((tm, RWKV_WIDTH), lambda i: (i, 0)),
            pl.BlockSpec((tm, RET_WIDTH), lambda i: (i, 0)),
            pl.BlockSpec((tm, d), lambda i: (i, GATE_OFF // d)),
            pl.BlockSpec((tm, d), lambda i: (i, GATE_OFF // d + 1)),
            pl.BlockSpec((tm, d), lambda i: (i, 0)),
            mod, pl.BlockSpec((1, d), const), mod, mod,
            pl.BlockSpec((RWKV_WIDTH, d), const, pipeline_mode=single),
            pl.BlockSpec((RET_WIDTH, d), const, pipeline_mode=single),
            pl.BlockSpec((d, d), const, pipeline_mode=single),
            pl.BlockSpec((d, LANES), const, pipeline_mode=single),
            pl.BlockSpec((d, LANES), const, pipeline_mode=single),
            pl.BlockSpec((1, LANES), const),
        ],
        out_specs=[pl.BlockSpec((tm, d), lambda i: (i, 0)), pl.BlockSpec((tm, d), lambda i: (i, 0)),
                   pl.BlockSpec((tm, LANES), lambda i: (i, 0))],
        out_shape=[jax.ShapeDtypeStruct((t, d), F32), jax.ShapeDtypeStruct((t, d), F32),
                   jax.ShapeDtypeStruct((t, LANES), F32)],
        compiler_params=_cparams(("parallel",)),
        name="merge_out_norm_router",
    )(ya, yb, proj, proj, x, gt1, n2g, sc2, sh2, wa, wb, wo, rwh, rwl, rb)


def _route_kernel(lg_ref, meta_ref, gate_ref, cnt_ref, carry_ref, *, tm):
    @pl.when(pl.program_id(0) == 0)
    def _():
        carry_ref[...] = jnp.zeros_like(carry_ref)

    l = lg_ref[...]
    lane = lax.broadcasted_iota(I32, l.shape, 1)
    lane_f = lane.astype(F32)
    vals, idxs = [], []
    for _ in range(TOP_K):
        m = jnp.max(l, axis=-1, keepdims=True)
        idx = jnp.min(jnp.where(l == m, lane_f, float(LANES)), axis=-1, keepdims=True)
        vals.append(m)
        idxs.append(idx)
        l = jnp.where(lane_f == idx, -jnp.inf, l)
    es = [jnp.exp(vv - vals[0]) for vv in vals]
    denom = es[0] + es[1] + es[2] + es[3]
    sel = jnp.zeros(l.shape, F32)
    for idx in idxs:
        sel = sel + jnp.where(lane_f == idx, 1.0, 0.0)
    row = lax.broadcasted_iota(I32, (tm, tm), 0)
    col = lax.broadcasted_iota(I32, (tm, tm), 1)
    tri = (col <= row).astype(BF16)
    cum = _dot(tri, sel.astype(BF16)) + carry_ref[0:1, :]
    excl = cum - sel
    carry_ref[0:1, :] = cum[tm - 1:tm, :]
    meta = jnp.zeros(l.shape, F32)
    gates = jnp.zeros(l.shape, F32)
    for kk in range(TOP_K):
        rank = jnp.sum(jnp.where(lane_f == idxs[kk], excl, 0.0), axis=-1, keepdims=True)
        meta = meta + jnp.where(lane == kk, idxs[kk], 0.0) + jnp.where(lane == TOP_K + kk, rank, 0.0)
        gates = gates + jnp.where(lane == kk, es[kk] / denom, 0.0)
    meta_ref[...] = meta.astype(I32)
    gate_ref[...] = gates
    cnt_ref[...] = jnp.broadcast_to(cum[tm - 1:tm, :], cnt_ref.shape).astype(I32)


def _route(logits):
    t = logits.shape[0]
    tm = 256
    return pl.pallas_call(
        functools.partial(_route_kernel, tm=tm),
        grid=(t // tm,),
        in_specs=[pl.BlockSpec((tm, LANES), lambda i: (i, 0))],
        out_specs=[pl.BlockSpec((tm, LANES), lambda i: (i, 0)), pl.BlockSpec((tm, LANES), lambda i: (i, 0)),
                   pl.BlockSpec((SUBLANES, LANES), lambda i: (0, 0))],
        out_shape=[jax.ShapeDtypeStruct((t, LANES), I32), jax.ShapeDtypeStruct((t, LANES), F32),
                   jax.ShapeDtypeStruct((SUBLANES, LANES), I32)],
        scratch_shapes=[pltpu.VMEM((SUBLANES, LANES), F32)],
        compiler_params=_cparams(("arbitrary",)),
        name="router_topk",
    )(logits)


def _moe_kernel(be_ref, nu_ref, tok_ref, h_hbm, wg_ref, wu_ref, bg_ref, bu_ref, wd_ref, bd_ref, ys_ref,
                xbuf, xb_s, wgu_s, wd_s, sem):
    i = pl.program_id(0)
    n_used = nu_ref[0]
    blk = MOE_ROWS
    f = D_EXPERT

    def row_copy(tok, r, slot):
        return pltpu.make_async_copy(h_hbm.at[pl.ds(tok, 1), :], xbuf.at[slot, pl.ds(r, 1), :], sem.at[slot])

    def wait_block(slot):
        pltpu.make_async_copy(h_hbm.at[pl.ds(0, blk), :], xbuf.at[slot], sem.at[slot]).wait()

    @pl.when(i == 0)
    def _():
        def body(r, carry):
            row_copy(tok_ref[r], r, 0).start()
            return carry
        lax.fori_loop(0, blk, body, 0)

    prev_e = be_ref[jnp.maximum(i - 1, 0)]

    @pl.when((i < n_used) & ((i == 0) | (be_ref[i] != prev_e)))
    def _():
        wgu_s[:, :f] = wg_ref[0, 0].astype(BF16)
        wgu_s[:, f:] = wu_ref[0, 0].astype(BF16)
        wd_s[...] = wd_ref[0, 0].astype(BF16)

    @pl.when(i < n_used)
    def _():
        slot = i % 2
        wait_block(slot)
        xb_s[...] = xbuf[slot].astype(BF16)
        base = (i + 1) * blk
        for r in range(blk):
            row_copy(tok_ref[base + r], r, 1 - slot).start()
        x = xb_s[...]
        gt = jnp.minimum(_dot(x, wgu_s[:, :f]) + bg_ref[0, 0], SWIGLU_LIMIT)
        up = jnp.clip(_dot(x, wgu_s[:, f:]) + bu_ref[0, 0], -SWIGLU_LIMIT, SWIGLU_LIMIT)
        act = (up + 1.0) * (gt * _sigmoid(gt * SWIGLU_ALPHA))
        ys_ref[...] = _dot(act.astype(BF16), wd_s[...]) + bd_ref[0, 0]

    @pl.when(i == n_used)
    def _():
        wait_block(i % 2)

    @pl.when(i >= n_used)
    def _():
        ys_ref[...] = jnp.zeros_like(ys_ref)


def _moe_experts(layer, blk_e, n_used, tok_tab, h2, w_gate, b_gate, w_up, b_up, w_down, b_down):
    d = h2.shape[1]
    f = D_EXPERT
    n_pad = tok_tab.shape[0]
    blk = MOE_ROWS
    nblk = n_pad // blk
    wmap = lambda i, be, nu, tk: (layer, be[i], 0, 0)
    grid_spec = pltpu.PrefetchScalarGridSpec(
        num_scalar_prefetch=3,
        grid=(nblk,),
        in_specs=[
            pl.BlockSpec(memory_space=pl.ANY),
            pl.BlockSpec((1, 1, d, f), wmap),
            pl.BlockSpec((1, 1, d, f), wmap),
            pl.BlockSpec((1, 1, 1, f), wmap),
            pl.BlockSpec((1, 1, 1, f), wmap),
            pl.BlockSpec((1, 1, f, d), wmap),
            pl.BlockSpec((1, 1, 1, d), wmap),
        ],
        out_specs=pl.BlockSpec((blk, d), lambda i, be, nu, tk: (i, 0)),
        scratch_shapes=[pltpu.VMEM((2, blk, d), F32), pltpu.VMEM((blk, d), BF16),
                        pltpu.VMEM((d, 2 * f), BF16), pltpu.VMEM((f, d), BF16),
                        pltpu.SemaphoreType.DMA((2,))],
    )
    n_layers, n_exp = w_gate.shape[:2]
    return pl.pallas_call(
        _moe_kernel,
        grid_spec=grid_spec,
        out_shape=jax.ShapeDtypeStruct((n_pad, d), F32),
        compiler_params=_cparams(("arbitrary",)),
        name="moe_experts",
    )(blk_e, n_used, tok_tab, h2, w_gate, w_up, b_gate.reshape(n_layers, n_exp, 1, f),
      b_up.reshape(n_layers, n_exp, 1, f), w_down, b_down.reshape(n_layers, n_exp, 1, d))


def _comb_kernel(dest_ref, ys_hbm, x_ref, gate_ref, gt2_ref, fg_ref, o_ref, buf, sem, *, final, n_tiles):
    i = pl.program_id(0)
    tc = COMB_ROWS

    def row_copy(base, r, kk, slot):
        return pltpu.make_async_copy(ys_hbm.at[pl.ds(dest_ref[base + r * TOP_K + kk], 1), :],
                                     buf.at[slot, kk, pl.ds(r, 1), :], sem.at[slot])

    def wait_tile(slot):
        for kk in range(TOP_K):
            pltpu.make_async_copy(ys_hbm.at[pl.ds(0, tc), :], buf.at[slot, kk], sem.at[slot]).wait()

    @pl.when(i == 0)
    def _():
        def body(r, carry):
            for kk in range(TOP_K):
                row_copy(0, r, kk, 0).start()
            return carry
        lax.fori_loop(0, tc, body, 0)

    slot = i % 2
    wait_tile(slot)
    base = jnp.minimum(i + 1, n_tiles - 1) * (tc * TOP_K)
    for r in range(tc):
        for kk in range(TOP_K):
            row_copy(base, r, kk, 1 - slot).start()
    gates = gate_ref[...]
    moe = gates[:, 0:1] * buf[slot, 0]
    for kk in range(1, TOP_K):
        moe = moe + gates[:, kk:kk + 1] * buf[slot, kk]
    xn = x_ref[...] + gt2_ref[0] * moe
    if final:
        xn = xn * lax.rsqrt(jnp.mean(xn * xn, axis=-1, keepdims=True) + NORM_EPS) * fg_ref[...]
    o_ref[...] = xn

    @pl.when(i == n_tiles - 1)
    def _():
        wait_tile(1 - slot)


def _moe_combine(dest_flat, ys, x, gates, gt2, final_g, seq, final):
    t, d = x.shape
    tc = COMB_ROWS
    nb = seq // tc
    n_tiles = t // tc
    grid_spec = pltpu.PrefetchScalarGridSpec(
        num_scalar_prefetch=1,
        grid=(n_tiles,),
        in_specs=[
            pl.BlockSpec(memory_space=pl.ANY),
            pl.BlockSpec((tc, d), lambda i, ds: (i, 0)),
            pl.BlockSpec((tc, LANES), lambda i, ds: (i, 0)),
            pl.BlockSpec((1, 1, d), lambda i, ds: (i // nb, 0, 0)),
            pl.BlockSpec((1, d), lambda i, ds: (0, 0)),
        ],
        out_specs=pl.BlockSpec((tc, d), lambda i, ds: (i, 0)),
        scratch_shapes=[pltpu.VMEM((2, TOP_K, tc, d), F32), pltpu.SemaphoreType.DMA((2,))],
    )
    return pl.pallas_call(
        functools.partial(_comb_kernel, final=final, n_tiles=n_tiles),
        grid_spec=grid_spec,
        out_shape=jax.ShapeDtypeStruct((t, d), F32),
        compiler_params=_cparams(("arbitrary",)),
        name="moe_combine",
    )(dest_flat, ys, x, gates, gt2, final_g)


def _pad_rows(w, n_rows, at=0):
    out = jnp.zeros((n_rows, w.shape[1]), w.dtype)
    return out.at[at:at + w.shape[0]].set(w)


def _layer_params(l, p):
    d = D_MODEL
    w_in = p["w_in"][l]
    sh, ret, gate = w_in[:, :N_SHIFT], w_in[:, N_SHIFT:N_SHIFT + N_RET], w_in[:, N_SHIFT + N_RET:]
    rkv = sh[:, :3 * RWKV_WIDTH]
    xw = sh[:, 3 * RWKV_WIDTH:3 * RWKV_WIDTH + LORA_DECAY]
    xa = sh[:, 3 * RWKV_WIDTH + LORA_DECAY:3 * RWKV_WIDTH + LORA_DECAY + LORA_A]
    xg = sh[:, 3 * RWKV_WIDTH + LORA_DECAY + LORA_A:]
    xg = jnp.pad(xg, ((0, 0), (0, XG_W - LORA_GATE)))
    mu = p["shift_mu"][l]
    mu_xg = jnp.pad(mu[3 * RWKV_WIDTH + LORA_DECAY + LORA_A:], (0, XG_W - LORA_GATE))
    mu_xv = jnp.concatenate([jnp.zeros((LORA_VRES,), F32), jnp.ones((LORA_VRES,), F32),
                             jnp.zeros((LANES - 2 * LORA_VRES,), F32)])
    if l == 0:
        xv = jnp.zeros((d, LANES), F32)
        v2 = jnp.zeros((LANES, RWKV_WIDTH), F32)
        v0 = jnp.zeros((RWKV_WIDTH,), F32)
    else:
        mu_v = p["vres_mu"][l - 1][:, None]
        v1 = p["vres_w1"][l - 1]
        xv = jnp.concatenate([v1 * (1.0 - mu_v), v1 * mu_v, jnp.zeros((d, LANES - 2 * LORA_VRES), F32)], axis=1)
        w2v = p["vres_w2"][l - 1]
        v2 = _pad_rows(jnp.concatenate([w2v, w2v], axis=0), LANES)
        v0 = p["vres_v0"][l - 1]
    w_proj = jnp.concatenate([ret, gate, rkv, xg, xw, xa, xv], axis=1).astype(BF16)
    mu_ext = jnp.concatenate([jnp.zeros((N_RET + 2 * d,), F32), mu[:3 * RWKV_WIDTH], mu_xg,
                              mu[3 * RWKV_WIDTH:3 * RWKV_WIDTH + LORA_DECAY + LORA_A], mu_xv])[None, :]
    row = lambda a: a.reshape(1, -1)
    pw = {
        "w0": row(p["rwkv_w0"][l]), "a0": row(p["rwkv_a0"][l]), "kks": row(p["rwkv_kk_scale"][l]),
        "ka": row(p["rwkv_k_a"][l]), "rk": row(p["rwkv_r_k"][l]), "lng": row(p["rwkv_lnx_g"][l]),
        "lnb": row(p["rwkv_lnx_b"][l]), "v0": row(v0),
        "w2": _pad_rows(p["rwkv_w2"][l], LANES, 0).astype(BF16),
        "a2": _pad_rows(p["rwkv_a2"][l], LANES, LORA_DECAY).astype(BF16),
        "g2": _pad_rows(p["rwkv_g2"][l], XG_W, 0).astype(BF16),
        "v2": v2.astype(BF16),
    }
    rw = jnp.pad(p["router_w"][l], ((0, 0), (0, LANES - N_EXPERTS)))
    rwh, rwl = _split_bf16(rw)
    rb = jnp.concatenate([p["router_b"][l], jnp.full((LANES - N_EXPERTS,), -1e30, F32)])[None, :]
    return {
        "w_proj": w_proj, "mu_ext": mu_ext, "pw": pw,
        "wa": p["w_up_a"][l].astype(BF16), "wb": p["w_up_b"][l].astype(BF16), "wo": p["w_out"][l].astype(BF16),
        "rwh": rwh, "rwl": rwl, "rb": rb,
        "n1g": row(p["norm1_g"][l]), "n2g": row(p["norm2_g"][l]), "gn_g": row(p["ret_gn_g"][l]),
    }


def _routing_tables(meta, counts, n_tok):
    blk = MOE_ROWS
    idx = meta[:, :TOP_K]
    rank = meta[:, TOP_K:2 * TOP_K]
    cnt = counts[0, :N_EXPERTS]
    pcnt = (cnt + blk - 1) // blk * blk
    pends = jnp.cumsum(pcnt)
    pstarts = pends - pcnt
    dest = (pstarts[idx] + rank).astype(I32)
    n_pad = n_tok * TOP_K + N_EXPERTS * blk
    nblk = n_pad // blk
    tok = jnp.broadcast_to(jnp.arange(n_tok, dtype=I32)[:, None], (n_tok, TOP_K))
    tok_tab = jnp.zeros((n_pad,), I32).at[dest.reshape(-1)].set(tok.reshape(-1), unique_indices=True)
    starts = jnp.arange(nblk, dtype=I32) * blk
    blk_e = jnp.minimum(jnp.sum((pends[None, :] <= starts[:, None]).astype(I32), axis=1), N_EXPERTS - 1)
    n_used = (pends[-1] // blk).astype(I32).reshape(1)
    return dest.reshape(-1), tok_tab, blk_e, n_used


def kernel(x, c, positions, ada_w, ada_b, norm1_g, norm2_g, w_in, shift_mu, rwkv_w0, rwkv_w2, rwkv_a0, rwkv_a2, rwkv_g2, rwkv_kk_scale, rwkv_k_a, rwkv_r_k, rwkv_lnx_g, rwkv_lnx_b, vres_mu, vres_w1, vres_w2, vres_v0, ret_gn_g, w_up_a, w_up_b, w_out, router_w, router_b, exp_w_gate, exp_b_gate, exp_w_up, exp_b_up, exp_w_down, exp_b_down, final_g):
    p = dict(w_in=w_in, shift_mu=shift_mu, rwkv_w0=rwkv_w0, rwkv_w2=rwkv_w2, rwkv_a0=rwkv_a0,
             rwkv_a2=rwkv_a2, rwkv_g2=rwkv_g2, rwkv_kk_scale=rwkv_kk_scale, rwkv_k_a=rwkv_k_a,
             rwkv_r_k=rwkv_r_k, rwkv_lnx_g=rwkv_lnx_g, rwkv_lnx_b=rwkv_lnx_b, vres_mu=vres_mu,
             vres_w1=vres_w1, vres_w2=vres_w2, vres_v0=vres_v0, ret_gn_g=ret_gn_g, w_up_a=w_up_a,
             w_up_b=w_up_b, w_out=w_out, router_w=router_w, router_b=router_b, norm1_g=norm1_g,
             norm2_g=norm2_g)
    batch, seq, d = x.shape
    n_layers = ada_w.shape[0]
    n_tok = batch * seq
    assert batch <= SUBLANES and d == D_MODEL

    c_pad = jnp.zeros((SUBLANES, d), F32).at[:batch].set(c)
    mod = _modulation(c_pad, ada_w, ada_b)[:, :batch].reshape(n_layers, batch, N_ADA, 1, d)

    half = RET_HEAD_DIM // 2
    inv_freq = ROPE_BASE ** (-jnp.arange(half, dtype=F32) / half)
    invf2 = jnp.concatenate([inv_freq, inv_freq])[None, :]
    cos2, sin2 = _rotary_tables(positions.reshape(n_tok, 1), invf2)
    log_gamma = jnp.log1p(-jnp.exp2(-5.0 - jnp.arange(RET_HEADS, dtype=F32)))
    log_gamma = jnp.broadcast_to(log_gamma[:, None, None], (RET_HEADS, 1, LANES))

    xf = x.reshape(n_tok, d)
    v_first = None
    for l in range(n_layers):
        lp = _layer_params(l, p)
        sh1, sc1, gt1, sh2, sc2, gt2 = (mod[l, :, i] for i in range(N_ADA))
        proj = _in_proj(xf, lp["n1g"], sc1, sh1, lp["w_proj"], seq)
        ya, v_first = _rwkv_mix(proj, lp["mu_ext"], lp["pw"], v_first, batch, seq)
        yb = _retention(proj, cos2, sin2, log_gamma, lp["gn_g"], batch, seq)
        xf, h2, logits = _merge_out(ya, yb, proj, xf, gt1, lp["n2g"], sc2, sh2, lp["wa"], lp["wb"],
                                    lp["wo"], lp["rwh"], lp["rwl"], lp["rb"], seq)
        meta, gates, counts = _route(logits)
        dest, tok_tab, blk_e, n_used = _routing_tables(meta, counts, n_tok)
        ys = _moe_experts(l, blk_e, n_used, tok_tab, h2, exp_w_gate, exp_b_gate, exp_w_up, exp_b_up,
                          exp_w_down, exp_b_down)
        xf = _moe_combine(dest, ys, xf, gates, gt2, final_g.reshape(1, d), seq, l == n_layers - 1)
    return xf.reshape(batch, seq, d)
```

```python
import functools
import math

import jax
import jax.numpy as jnp
from jax import lax
from jax.experimental import pallas as pl
from jax.experimental.pallas import tpu as pltpu

F32 = jnp.float32
BF16 = jnp.bfloat16
I32 = jnp.int32

D_MODEL = 2048
RWKV_HEAD_DIM = 64
RWKV_WIDTH = 1024
LORA_DECAY = 64
LORA_A = 64
LORA_VRES = 32
LORA_GATE = 160
RWKV_GN_EPS = 64e-5
RET_HEADS = 8
RET_HEAD_DIM = 128
RET_WIDTH = 1024
RET_GN_EPS = 1e-5
ROPE_BASE = 10000.0
N_EXPERTS = 32
TOP_K = 4
D_EXPERT = 512
SWIGLU_ALPHA = 1.702
SWIGLU_LIMIT = 7.0
NORM_EPS = 1e-6
N_ADA = 6
N_SHIFT = 3 * RWKV_WIDTH + LORA_DECAY + LORA_A + LORA_GATE
N_RET = 4 * RET_WIDTH

LANES = 128
SUBLANES = 8
MXU_DIM = 256

RET_OFF = 0
GATE_OFF = RET_OFF + N_RET
RW_OFF = GATE_OFF + 2 * D_MODEL
XG_OFF = RW_OFF + 3 * RWKV_WIDTH
XG_W = 256
XWA_OFF = XG_OFF + XG_W
XV_OFF = XWA_OFF + LANES
N_PROJ = XV_OFF + LANES

RWKV_CHUNK = 64
RWKV_ROWS = 256
RWKV_GROUPS_PER_BODY = 4
RET_ROWS = 256
MOE_ROWS = 256
COMB_ROWS = 128
VMEM_LIMIT = 56 * 1024 * 1024


def _dot(a, b):
    return jnp.dot(a, b, preferred_element_type=F32)


def _dot_nt(a, b):
    return lax.dot_general(a, b, (((1,), (1,)), ((), ())), preferred_element_type=F32)


def _split_bf16(x):
    hi = x.astype(BF16)
    lo = (x - hi.astype(F32)).astype(BF16)
    return hi, lo


def _sigmoid(x):
    return 1.0 / (1.0 + jnp.exp(-x))


def _cparams(sem):
    return pltpu.CompilerParams(dimension_semantics=sem, vmem_limit_bytes=VMEM_LIMIT)


SLAB_W = LANES
SLAB_ROWS = D_MODEL // SLAB_W


def _store_slabs(ref, val):
    n = val.shape[0]
    for s in range(SLAB_ROWS):
        ref[pl.ds(s, n, stride=SLAB_ROWS), :] = val[:, s * SLAB_W:(s + 1) * SLAB_W]


def _load_slab_piece(ref_view, s, n):
    return ref_view[pl.ds(s, n, stride=SLAB_ROWS), :]


def _mod_kernel(c_ref, w_ref, b_ref, o_ref):
    c = c_ref[...]
    cs = c * _sigmoid(c)
    o_ref[0] = _dot(cs.astype(BF16), w_ref[0].astype(BF16)) + b_ref[0]


def _modulation(c_pad, ada_w, ada_b):
    n_layers, d, n = ada_w.shape
    tn = 1024
    return pl.pallas_call(
        _mod_kernel,
        grid=(n_layers, n // tn),
        in_specs=[
            pl.BlockSpec((SUBLANES, d), lambda l, j: (0, 0)),
            pl.BlockSpec((1, d, tn), lambda l, j: (l, 0, j)),
            pl.BlockSpec((1, 1, tn), lambda l, j: (l, 0, j)),
        ],
        out_specs=pl.BlockSpec((1, SUBLANES, tn), lambda l, j: (l, 0, j)),
        out_shape=jax.ShapeDtypeStruct((n_layers, SUBLANES, n), F32),
        compiler_params=_cparams(("parallel", "parallel")),
        name="adaln_mod",
    )(c_pad, ada_w, ada_b.reshape(n_layers, 1, n))


def _trig_kernel(pos_ref, invf_ref, cos_ref, sin_ref):
    ang = pos_ref[...].astype(F32) * invf_ref[...]
    lane = lax.broadcasted_iota(I32, ang.shape, 1)
    s = jnp.sin(ang)
    cos_ref[...] = jnp.cos(ang)
    sin_ref[...] = jnp.where(lane < RET_HEAD_DIM // 2, -s, s)


def _rotary_tables(pos_col, invf2):
    t = pos_col.shape[0]
    tm = 1024
    return pl.pallas_call(
        _trig_kernel,
        grid=(t // tm,),
        in_specs=[pl.BlockSpec((tm, 1), lambda i: (i, 0)), pl.BlockSpec((1, LANES), lambda i: (0, 0))],
        out_specs=[pl.BlockSpec((tm, LANES), lambda i: (i, 0))] * 2,
        out_shape=[jax.ShapeDtypeStruct((t, LANES), F32)] * 2,
        compiler_params=_cparams(("parallel",)),
        name="rotary_tables",
    )(pos_col, invf2)


def _in_kernel(x_ref, g_ref, sc_ref, sh_ref, w_ref, o_ref, h_scr):
    @pl.when(pl.program_id(1) == 0)
    def _():
        x = x_ref[...]
        y = x * lax.rsqrt(jnp.mean(x * x, axis=-1, keepdims=True) + NORM_EPS)
        h_scr[...] = (y * g_ref[...] * (1.0 + sc_ref[0]) + sh_ref[0]).astype(BF16)

    o_ref[...] = _dot(h_scr[...], w_ref[...])


def _in_proj(x, g, sc, sh, w, seq):
    t, d = x.shape
    n = w.shape[1]
    tm = min(1024, seq)
    tn = 512
    return pl.pallas_call(
        _in_kernel,
        grid=(t // tm, n // tn),
        in_specs=[
            pl.BlockSpec((tm, d), lambda i, j: (i, 0)),
            pl.BlockSpec((1, d), lambda i, j: (0, 0)),
            pl.BlockSpec((1, 1, d), lambda i, j: (i * tm // seq, 0, 0)),
            pl.BlockSpec((1, 1, d), lambda i, j: (i * tm // seq, 0, 0)),
            pl.BlockSpec((d, tn), lambda i, j: (0, j)),
        ],
        out_specs=pl.BlockSpec((tm, tn), lambda i, j: (i, j)),
        out_shape=jax.ShapeDtypeStruct((t, n), F32),
        scratch_shapes=[pltpu.VMEM((tm, d), BF16)],
        compiler_params=_cparams(("parallel", "arbitrary")),
        name="norm_in_proj",
    )(x, g, sc, sh, w)


def _bdot(a, b):
    return jnp.einsum("bij,bjk->bik", a, b, preferred_element_type=F32)


def _bdot_nt(a, b):
    return jnp.einsum("bik,bjk->bij", a, b, preferred_element_type=F32)


def _rwkv_chunk_terms(at, rt, bt, kt, bh, kh, v, wc_row):
    c = RWKV_CHUNK
    n = RWKV_HEAD_DIM
    lw = LANES
    nb = at.shape[0]
    reps = lw // c
    row = lax.broadcasted_iota(I32, (nb, lw, lw), 1)
    col = lax.broadcasted_iota(I32, (nb, lw, lw), 2)
    same_head = (row // n) == (col // n)

    def stack(x):
        return jnp.where(same_head, jnp.concatenate([x] * reps, axis=1), 0.0)

    def transpose(x):
        return jnp.stack([x[i].T for i in range(nb)])

    at2 = stack(at)
    rt2 = stack(rt)
    v2 = stack(v).astype(BF16)
    lhs = jnp.concatenate([at2, rt2], axis=1).astype(BF16)
    rhs = jnp.concatenate([stack(bt), stack(kt)], axis=1).astype(BF16)
    aa = _bdot_nt(lhs, rhs)
    strict = row > col
    lower = row >= col
    a_ab = jnp.where(strict, aa[:, :lw, :lw], 0.0)
    a_ak = jnp.where(strict, aa[:, :lw, lw:], 0.0)
    a_rb = jnp.where(lower, aa[:, lw:, :lw], 0.0)
    a_rk = jnp.where(lower, aa[:, lw:, lw:], 0.0)

    x = jnp.where(row == col, 1.0, 0.0) + jnp.where(((row ^ col) == 1) & ((row & 1) == 1), a_ab, 0.0)
    b = 2
    while b < c:
        sel = ((row // (2 * b)) == (col // (2 * b))) & ((row & b) != 0) & ((col & b) == 0)
        xb = x.astype(BF16)
        y = _bdot(xb, jnp.where(sel, a_ab, 0.0).astype(BF16))
        x = x + _bdot(y.astype(BF16), xb)
        b *= 2

    akv = _bdot(a_ak.astype(BF16), v2)
    p12b = _bdot(x.astype(BF16), jnp.concatenate([at2, akv], axis=2).astype(BF16)).astype(BF16)
    arb = _bdot(a_rb.astype(BF16), p12b)
    qp = (rt2 + arb[:, :, :lw]).astype(BF16)
    o_in = arb[:, :, lw:] + _bdot(a_rk.astype(BF16), v2)
    bp = _bdot(transpose(stack(bh)).astype(BF16), p12b)
    ht = bp[:, :, lw:] + _bdot(transpose(stack(kh)).astype(BF16), v2)
    wc_col = jnp.sum(jnp.where(row == col, wc_row, 0.0), axis=2, keepdims=True)
    return qp, o_in, bp[:, :, :lw].astype(BF16), ht, wc_col


def _rwkv_kernel(*refs, rows, has_vres):
    c = RWKV_CHUNK
    n = RWKV_HEAD_DIM
    lw = LANES
    ng = RWKV_WIDTH // lw
    nc = rows // c
    gpb = RWKV_GROUPS_PER_BODY
    it = iter(refs)
    r_ref, k_ref, v_ref, xg_ref, xwa_ref, xv_ref = (next(it) for _ in range(6))
    mu_r, mu_k, mu_v, mu_xg, mu_xwa, mu_xv = (next(it) for _ in range(6))
    w0, a0, kks, ka, rk, lng, lnb, v0 = (next(it) for _ in range(8))
    w2_ref, a2_ref, g2_ref, v2_ref = (next(it) for _ in range(4))
    vf_ref = next(it) if has_vres else None
    ya_ref = next(it)
    vfo_ref = None if has_vres else next(it)
    c_r, c_k, c_v, c_xg, c_xwa, c_xv = (next(it) for _ in range(6))
    st_s = next(it)
    at_s, rt_s, bt_s, kt_s, bh_s, kh_s, vv_s, wc_s, o_s = (next(it) for _ in range(9))
    bonus_s, gate_s = next(it), next(it)

    @pl.when(pl.program_id(1) == 0)
    def _():
        for cr in (c_r, c_k, c_v, c_xg, c_xwa, c_xv):
            cr[...] = jnp.zeros_like(cr)
        st_s[...] = jnp.zeros_like(st_s)

    def shift(p_ref, mu_ref, carry_ref):
        p = p_ref[...]
        row = lax.broadcasted_iota(I32, p.shape, 0)
        prev = jnp.where(row == 0, carry_ref[0:1, :], pltpu.roll(p, 1, 0))
        carry_ref[0:1, :] = p[rows - 1:rows, :]
        return p + (prev - p) * mu_ref[...]

    r = shift(r_ref, mu_r, c_r)
    k = shift(k_ref, mu_k, c_k)
    v = shift(v_ref, mu_v, c_v)
    xg = shift(xg_ref, mu_xg, c_xg)
    xwa = shift(xwa_ref, mu_xwa, c_xwa)

    z = w0[...] + _dot(jnp.tanh(xwa).astype(BF16), w2_ref[...])
    lw_log = -math.exp(-0.5) * _sigmoid(z)
    a = _sigmoid(a0[...] + _dot(xwa.astype(BF16), a2_ref[...]))
    gate_s[...] = _dot(_sigmoid(xg).astype(BF16), g2_ref[...])
    if has_vres:
        xv = shift(xv_ref, mu_xv, c_xv)
        v = v + (vf_ref[...] - v) * _sigmoid(v0[...] + _dot(xv.astype(BF16), v2_ref[...]))
    else:
        vfo_ref[...] = v

    hrow = lax.broadcasted_iota(I32, (MXU_DIM, MXU_DIM), 0) // n
    hcol = lax.broadcasted_iota(I32, (MXU_DIM, MXU_DIM), 1) // n
    head_ones = (hrow == hcol).astype(BF16)

    def headsum(x):
        xb = x.astype(BF16)
        return jnp.concatenate([_dot(xb[:, i * MXU_DIM:(i + 1) * MXU_DIM], head_ones)
                                for i in range(RWKV_WIDTH // MXU_DIM)], axis=1)

    kk0 = k * kks[...]
    kk = kk0 * lax.rsqrt(jnp.maximum(headsum(kk0 * kk0), 1e-24))
    km = k * (1.0 + (a - 1.0) * ka[...])
    bonus_s[...] = headsum(r * km * rk[...]) * v

    trow = lax.broadcasted_iota(I32, (rows, rows), 0) // c
    tcol = lax.broadcasted_iota(I32, (rows, rows), 1) // c
    same_chunk = trow == tcol
    tri = (same_chunk & (lax.broadcasted_iota(I32, (rows, rows), 1)
                         <= lax.broadcasted_iota(I32, (rows, rows), 0))).astype(BF16)
    blk_ones = same_chunk.astype(BF16)
    lhi, llo = _split_bf16(lw_log)
    cs = _dot(tri, lhi) + _dot(tri, llo)
    ctot = _dot(blk_ones, lhi) + _dot(blk_ones, llo)
    b_vec = kk * a
    w_inv = jnp.exp(-cs)
    tail = jnp.exp(ctot - cs)
    planes = ((at_s, -kk * jnp.exp(cs - lw_log)), (rt_s, r * jnp.exp(cs)), (bt_s, b_vec * w_inv),
              (kt_s, km * w_inv), (bh_s, b_vec * tail), (kh_s, km * tail), (vv_s, v), (wc_s, jnp.exp(ctot)))
    for dst, val in planes:
        for g in range(ng):
            dst[g] = val[:, g * lw:(g + 1) * lw]

    def pair_body(gp, carry):
        gs = pl.ds(gp * gpb, gpb)

        def chunks(ref):
            return ref[gs].reshape(gpb * nc, c, lw)

        qp_b, o_in_b, bpg_b, ht_b, wc_col_b = _rwkv_chunk_terms(
            chunks(at_s), chunks(rt_s), chunks(bt_s), chunks(kt_s), chunks(bh_s), chunks(kh_s),
            chunks(vv_s), chunks(wc_s)[:, 0:1, :])
        states = [st_s[gp * gpb + gg] for gg in range(gpb)]
        for ci in range(nc):
            for gg in range(gpb):
                bi = gg * nc + ci
                qp, o_in, bpg, ht, wc_col = qp_b[bi], o_in_b[bi], bpg_b[bi], ht_b[bi], wc_col_b[bi]
                st = states[gg]
                stb = st.astype(BF16)
                o2 = _dot(qp, stb) + o_in
                o = o2[0:c, :]
                for h in range(1, lw // c):
                    o = o + o2[h * c:(h + 1) * c, :]
                o_s[gp * gpb + gg, pl.ds(ci * c, c), :] = o
                states[gg] = wc_col * st + _dot(bpg, stb) + ht
        for gg in range(gpb):
            st_s[gp * gpb + gg] = states[gg]
        return carry

    lax.fori_loop(0, ng // gpb, pair_body, 0)

    o = jnp.concatenate([o_s[g] for g in range(ng)], axis=1)
    mean = headsum(o) * (1.0 / n)
    dev = o - mean
    var = headsum(dev * dev) * (1.0 / n)
    on = dev * lax.rsqrt(var + RWKV_GN_EPS) * lng[...] + lnb[...]
    ya_ref[...] = ((on + bonus_s[...]) * gate_s[...]).astype(ya_ref.dtype)


def _rwkv_mix(proj, mu_ext, pw, v_first, batch, seq):
    t = proj.shape[0]
    rows = min(RWKV_ROWS, seq)
    w = RWKV_WIDTH
    nj = seq // rows
    ng = w // LANES
    has_vres = v_first is not None

    def colspec(width, off):
        return pl.BlockSpec((rows, width), lambda b, j: (b * nj + j, off // width))

    def muspec(width, off):
        return pl.BlockSpec((1, width), lambda b, j: (0, off // width))

    layout = [(w, RW_OFF), (w, RW_OFF + w), (w, RW_OFF + 2 * w), (XG_W, XG_OFF), (LANES, XWA_OFF), (LANES, XV_OFF)]
    in_specs = [colspec(*a) for a in layout] + [muspec(*a) for a in layout]
    args = [proj] * 6 + [mu_ext] * 6
    for name in ("w0", "a0", "kks", "ka", "rk", "lng", "lnb", "v0"):
        in_specs.append(pl.BlockSpec((1, w), lambda b, j: (0, 0)))
        args.append(pw[name])
    for name, kdim in (("w2", LANES), ("a2", LANES), ("g2", XG_W), ("v2", LANES)):
        in_specs.append(pl.BlockSpec((kdim, w), lambda b, j: (0, 0)))
        args.append(pw[name])
    io_spec = pl.BlockSpec((rows, w), lambda b, j: (b * nj + j, 0))
    out_specs = [io_spec]
    out_shape = [jax.ShapeDtypeStruct((t, w), BF16)]
    if has_vres:
        in_specs.append(io_spec)
        args.append(v_first)
    else:
        out_specs.append(io_spec)
        out_shape.append(jax.ShapeDtypeStruct((t, w), F32))
    scratch = [pltpu.VMEM((SUBLANES, w), F32)] * 3 + [pltpu.VMEM((SUBLANES, XG_W), F32)]
    scratch += [pltpu.VMEM((SUBLANES, LANES), F32)] * 2
    scratch += [pltpu.VMEM((ng, LANES, LANES), F32)]
    scratch += [pltpu.VMEM((ng, rows, LANES), F32)] * 9
    scratch += [pltpu.VMEM((rows, w), F32)] * 2
    outs = pl.pallas_call(
        functools.partial(_rwkv_kernel, rows=rows, has_vres=has_vres),
        grid=(batch, nj),
        in_specs=in_specs,
        out_specs=out_specs,
        out_shape=out_shape,
        scratch_shapes=scratch,
        compiler_params=_cparams(("parallel", "arbitrary")),
        name="rwkv7_mix",
    )(*args)
    if has_vres:
        return outs[0], v_first
    return outs[0], outs[1]


def _ret_kernel(q_ref, k_ref, v_ref, g_ref, cos_ref, sin_ref, lg_ref, gn_ref, o_ref, st_ref, *, rows):
    @pl.when(pl.program_id(1) == 0)
    def _():
        st_ref[...] = jnp.zeros_like(st_ref)

    hd = RET_HEAD_DIM
    cos = cos_ref[...]
    sin = sin_ref[...]
    row = lax.broadcasted_iota(I32, (rows, rows), 0)
    col = lax.broadcasted_iota(I32, (rows, rows), 1)
    causal = row >= col
    rel = jnp.maximum(row - col, 0).astype(F32)
    idx = lax.broadcasted_iota(I32, (rows, LANES), 0).astype(F32)

    def rot(x):
        return x * cos + pltpu.roll(x, hd // 2, 1) * sin

    for h in range(RET_HEADS):
        hs = slice(h * hd, (h + 1) * hd)
        lg = lg_ref[h]
        q = rot(q_ref[:, hs])
        k = rot(k_ref[:, hs]) * (hd ** -0.5)
        v = v_ref[:, hs].astype(BF16)
        lg_sq = jnp.concatenate([lg] * (rows // LANES), axis=1) if rows > LANES else lg
        decay = jnp.where(causal, jnp.exp(rel * lg_sq), 0.0)
        qb = q.astype(BF16)
        scores = _dot_nt(qb, k.astype(BF16)) * decay
        intra = _dot(scores.astype(BF16), v)
        xi = jnp.exp((idx + 1.0) * lg)
        st = st_ref[h]
        cross = _dot(qb, st.astype(BF16)) * xi
        zeta = jnp.exp((rows - 1.0 - idx) * lg)
        kz = (k * zeta).T.astype(BF16)
        chunk_decay = jnp.exp(rows * lg)
        st_ref[h] = st * chunk_decay + _dot(kz, v)
        o = intra + cross
        mean = jnp.mean(o, axis=-1, keepdims=True)
        dev = o - mean
        var = jnp.mean(dev * dev, axis=-1, keepdims=True)
        on = dev * lax.rsqrt(var + RET_GN_EPS) * gn_ref[:, hs]
        gr = g_ref[:, hs]
        o_ref[:, hs] = (gr * _sigmoid(gr) * on).astype(o_ref.dtype)


def _retention(proj, cos2, sin2, log_gamma, gn_g, batch, seq):
    t = proj.shape[0]
    rows = min(RET_ROWS, seq)
    nj = seq // rows
    w = RET_WIDTH

    def pspec(off):
        return pl.BlockSpec((rows, w), lambda b, j: (b * nj + j, off // w))

    tspec = pl.BlockSpec((rows, LANES), lambda b, j: (b * nj + j, 0))
    return pl.pallas_call(
        functools.partial(_ret_kernel, rows=rows),
        grid=(batch, nj),
        in_specs=[pspec(RET_OFF), pspec(RET_OFF + w), pspec(RET_OFF + 2 * w), pspec(RET_OFF + 3 * w),
                  tspec, tspec,
                  pl.BlockSpec((RET_HEADS, 1, LANES), lambda b, j: (0, 0, 0)),
                  pl.BlockSpec((1, w), lambda b, j: (0, 0))],
        out_specs=pl.BlockSpec((rows, w), lambda b, j: (b * nj + j, 0)),
        out_shape=jax.ShapeDtypeStruct((t, w), BF16),
        scratch_shapes=[pltpu.VMEM((RET_HEADS, RET_HEAD_DIM, RET_HEAD_DIM), F32)],
        compiler_params=_cparams(("parallel", "arbitrary")),
        name="retention",
    )(proj, proj, proj, proj, cos2, sin2, log_gamma, gn_g)


def _out_kernel(ya_ref, yb_ref, ga_ref, gb_ref, x_ref, gt1_ref, n2g_ref, sc2_ref, sh2_ref,
                wa_ref, wb_ref, wo_ref, rwh_ref, rwl_ref, rb_ref, xo_ref, h2_ref, lg_ref):
    ua = _dot(ya_ref[...], wa_ref[...])
    ub = _dot(yb_ref[...], wb_ref[...])
    y = _sigmoid(ga_ref[...]) * ua + _sigmoid(gb_ref[...]) * ub
    xn = x_ref[...] + gt1_ref[0] * _dot(y.astype(BF16), wo_ref[...])
    xo_ref[...] = xn
    yn = xn * lax.rsqrt(jnp.mean(xn * xn, axis=-1, keepdims=True) + NORM_EPS)
    h2 = yn * n2g_ref[...] * (1.0 + sc2_ref[0]) + sh2_ref[0]
    _store_slabs(h2_ref, h2)
    hi, lo = _split_bf16(h2)
    lg_ref[...] = _dot(hi, rwh_ref[...]) + _dot(hi, rwl_ref[...]) + _dot(lo, rwh_ref[...]) + rb_ref[...]


def _merge_out(ya, yb, proj, x, gt1, n2g, sc2, sh2, wa, wb, wo, rwh, rwl, rb, seq):
    t, d = x.shape
    tm = 256
    nb = seq // tm
    const = lambda i: (0, 0)
    mod = pl.BlockSpec((1, 1, d), lambda i: (i // nb, 0, 0))
    single = pl.Buffered(1)
    return pl.pallas_call(
        _out_kernel,
        grid=(t // tm,),
        in_specs=[
            pl.BlockSpec((tm, RWKV_WIDTH), lambda i: (i, 0)),
            pl.BlockSpec((tm, RET_WIDTH), lambda i: (i, 0)),
            pl.BlockSpec((tm, d), lambda i: (i, GATE_OFF // d)),
            pl.BlockSpec((tm, d), lambda i: (i, GATE_OFF // d + 1)),
            pl.BlockSpec((tm, d), lambda i: (i, 0)),
            mod, pl.BlockSpec((1, d), const), mod, mod,
            pl.BlockSpec((RWKV_WIDTH, d), const, pipeline_mode=single),
            pl.BlockSpec((RET_WIDTH, d), const, pipeline_mode=single),
            pl.BlockSpec((d, d), const, pipeline_mode=single),
            pl.BlockSpec((d, LANES), const, pipeline_mode=single),
            pl.BlockSpec((d, LANES), const, pipeline_mode=single),
            pl.BlockSpec((1, LANES), const),
        ],
        out_specs=[pl.BlockSpec((tm, d), lambda i: (i, 0)),
                   pl.BlockSpec((tm * SLAB_ROWS, SLAB_W), lambda i: (i, 0)),
                   pl.BlockSpec((tm, LANES), lambda i: (i, 0))],
        out_shape=[jax.ShapeDtypeStruct((t, d), F32), jax.ShapeDtypeStruct((t * SLAB_ROWS, SLAB_W), F32),
                   jax.ShapeDtypeStruct((t, LANES), F32)],
        compiler_params=_cparams(("parallel",)),
        name="merge_out_norm_router",
    )(ya, yb, proj, proj, x, gt1, n2g, sc2, sh2, wa, wb, wo, rwh, rwl, rb)


def _route_kernel(lg_ref, meta_ref, gate_ref, cnt_ref, carry_ref, *, tm):
    @pl.when(pl.program_id(0) == 0)
    def _():
        carry_ref[...] = jnp.zeros_like(carry_ref)

    l = lg_ref[...]
    lane = lax.broadcasted_iota(I32, l.shape, 1)
    lane_f = lane.astype(F32)
    vals, idxs = [], []
    for _ in range(TOP_K):
        m = jnp.max(l, axis=-1, keepdims=True)
        idx = jnp.min(jnp.where(l == m, lane_f, float(LANES)), axis=-1, keepdims=True)
        vals.append(m)
        idxs.append(idx)
        l = jnp.where(lane_f == idx, -jnp.inf, l)
    es = [jnp.exp(vv - vals[0]) for vv in vals]
    denom = es[0] + es[1] + es[2] + es[3]
    sel = jnp.zeros(l.shape, F32)
    for idx in idxs:
        sel = sel + jnp.where(lane_f == idx, 1.0, 0.0)
    row = lax.broadcasted_iota(I32, (tm, tm), 0)
    col = lax.broadcasted_iota(I32, (tm, tm), 1)
    tri = (col <= row).astype(BF16)
    cum = _dot(tri, sel.astype(BF16)) + carry_ref[0:1, :]
    excl = cum - sel
    carry_ref[0:1, :] = cum[tm - 1:tm, :]
    meta = jnp.zeros(l.shape, F32)
    gates = jnp.zeros(l.shape, F32)
    for kk in range(TOP_K):
        rank = jnp.sum(jnp.where(lane_f == idxs[kk], excl, 0.0), axis=-1, keepdims=True)
        meta = meta + jnp.where(lane == kk, idxs[kk], 0.0) + jnp.where(lane == TOP_K + kk, rank, 0.0)
        gates = gates + jnp.where(lane == kk, es[kk] / denom, 0.0)
    meta_ref[...] = meta.astype(I32)
    gate_ref[...] = gates
    cnt_ref[...] = jnp.broadcast_to(cum[tm - 1:tm, :], cnt_ref.shape).astype(I32)


def _route(logits):
    t = logits.shape[0]
    tm = 256
    return pl.pallas_call(
        functools.partial(_route_kernel, tm=tm),
        grid=(t // tm,),
        in_specs=[pl.BlockSpec((tm, LANES), lambda i: (i, 0))],
        out_specs=[pl.BlockSpec((tm, LANES), lambda i: (i, 0)), pl.BlockSpec((tm, LANES), lambda i: (i, 0)),
                   pl.BlockSpec((SUBLANES, LANES), lambda i: (0, 0))],
        out_shape=[jax.ShapeDtypeStruct((t, LANES), I32), jax.ShapeDtypeStruct((t, LANES), F32),
                   jax.ShapeDtypeStruct((SUBLANES, LANES), I32)],
        scratch_shapes=[pltpu.VMEM((SUBLANES, LANES), F32)],
        compiler_params=_cparams(("arbitrary",)),
        name="router_topk",
    )(logits)


def _moe_kernel(be_ref, nu_ref, tok_ref, h_hbm, wg_ref, wu_ref, bg_ref, bu_ref, wd_ref, bd_ref, ys_ref,
                xbuf, xb_s, wgu_s, wd_s, sem):
    i = pl.program_id(0)
    n_used = nu_ref[0]
    blk = MOE_ROWS
    f = D_EXPERT

    def row_copy(tok, r, slot):
        return pltpu.make_async_copy(h_hbm.at[pl.ds(pl.multiple_of(tok * SLAB_ROWS, SLAB_ROWS), SLAB_ROWS), :],
                                     xbuf.at[slot, pl.ds(r * SLAB_ROWS, SLAB_ROWS), :], sem.at[slot])

    def wait_block(slot):
        pltpu.make_async_copy(h_hbm.at[pl.ds(0, blk * SLAB_ROWS), :], xbuf.at[slot], sem.at[slot]).wait()

    @pl.when(i == 0)
    def _():
        def body(r, carry):
            row_copy(tok_ref[r], r, 0).start()
            return carry
        lax.fori_loop(0, blk, body, 0)

    prev_e = be_ref[jnp.maximum(i - 1, 0)]

    @pl.when((i < n_used) & ((i == 0) | (be_ref[i] != prev_e)))
    def _():
        wgu_s[:, :f] = wg_ref[0, 0].astype(BF16)
        wgu_s[:, f:] = wu_ref[0, 0].astype(BF16)
        wd_s[...] = wd_ref[0, 0].astype(BF16)

    @pl.when(i < n_used)
    def _():
        slot = i % 2
        wait_block(slot)
        for s in range(SLAB_ROWS):
            xb_s[:, s * SLAB_W:(s + 1) * SLAB_W] = _load_slab_piece(xbuf.at[slot], s, blk).astype(BF16)
        base = (i + 1) * blk
        for r in range(blk):
            row_copy(tok_ref[base + r], r, 1 - slot).start()
        x = xb_s[...]
        gt = jnp.minimum(_dot(x, wgu_s[:, :f]) + bg_ref[0, 0], SWIGLU_LIMIT)
        up = jnp.clip(_dot(x, wgu_s[:, f:]) + bu_ref[0, 0], -SWIGLU_LIMIT, SWIGLU_LIMIT)
        act = (up + 1.0) * (gt * _sigmoid(gt * SWIGLU_ALPHA))
        _store_slabs(ys_ref, _dot(act.astype(BF16), wd_s[...]) + bd_ref[0, 0])

    @pl.when(i == n_used)
    def _():
        wait_block(i % 2)

    @pl.when(i >= n_used)
    def _():
        ys_ref[...] = jnp.zeros_like(ys_ref)


def _moe_experts(layer, blk_e, n_used, tok_tab, h2, w_gate, b_gate, w_up, b_up, w_down, b_down):
    d = D_MODEL
    f = D_EXPERT
    n_pad = tok_tab.shape[0]
    blk = MOE_ROWS
    nblk = n_pad // blk
    wmap = lambda i, be, nu, tk: (layer, be[i], 0, 0)
    grid_spec = pltpu.PrefetchScalarGridSpec(
        num_scalar_prefetch=3,
        grid=(nblk,),
        in_specs=[
            pl.BlockSpec(memory_space=pl.ANY),
            pl.BlockSpec((1, 1, d, f), wmap),
            pl.BlockSpec((1, 1, d, f), wmap),
            pl.BlockSpec((1, 1, 1, f), wmap),
            pl.BlockSpec((1, 1, 1, f), wmap),
            pl.BlockSpec((1, 1, f, d), wmap),
            pl.BlockSpec((1, 1, 1, d), wmap),
        ],
        out_specs=pl.BlockSpec((blk * SLAB_ROWS, SLAB_W), lambda i, be, nu, tk: (i, 0)),
        scratch_shapes=[pltpu.VMEM((2, blk * SLAB_ROWS, SLAB_W), F32), pltpu.VMEM((blk, d), BF16),
                        pltpu.VMEM((d, 2 * f), BF16), pltpu.VMEM((f, d), BF16),
                        pltpu.SemaphoreType.DMA((2,))],
    )
    n_layers, n_exp = w_gate.shape[:2]
    return pl.pallas_call(
        _moe_kernel,
        grid_spec=grid_spec,
        out_shape=jax.ShapeDtypeStruct((n_pad * SLAB_ROWS, SLAB_W), F32),
        compiler_params=_cparams(("arbitrary",)),
        name="moe_experts",
    )(blk_e, n_used, tok_tab, h2, w_gate, w_up, b_gate.reshape(n_layers, n_exp, 1, f),
      b_up.reshape(n_layers, n_exp, 1, f), w_down, b_down.reshape(n_layers, n_exp, 1, d))


def _comb_kernel(dest_ref, ys_hbm, x_ref, gate_ref, gt2_ref, fg_ref, o_ref, buf_a, buf_b, sem, *, final, n_tiles):
    i = pl.program_id(0)
    tc = COMB_ROWS

    def row_copy(base, r, kk, buf, slot):
        src = pl.multiple_of(dest_ref[base + r * TOP_K + kk] * SLAB_ROWS, SLAB_ROWS)
        return pltpu.make_async_copy(ys_hbm.at[pl.ds(src, SLAB_ROWS), :],
                                     buf.at[kk, pl.ds(r * SLAB_ROWS, SLAB_ROWS), :], sem.at[slot])

    def wait_tile(buf, slot):
        for kk in range(TOP_K):
            pltpu.make_async_copy(ys_hbm.at[pl.ds(0, tc * SLAB_ROWS), :], buf.at[kk], sem.at[slot]).wait()

    @pl.when(i == 0)
    def _():
        def body(r, carry):
            for kk in range(TOP_K):
                row_copy(0, r, kk, buf_a, 0).start()
            return carry
        lax.fori_loop(0, tc, body, 0)

    def step(cur, cur_slot, nxt, nxt_slot):
        wait_tile(cur, cur_slot)
        base = jnp.minimum(i + 1, n_tiles - 1) * (tc * TOP_K)
        for r in range(tc):
            for kk in range(TOP_K):
                row_copy(base, r, kk, nxt, nxt_slot).start()
        gates = gate_ref[...]
        pieces = []
        for s in range(SLAB_ROWS):
            piece = gates[:, 0:1] * _load_slab_piece(cur.at[0], s, tc)
            for kk in range(1, TOP_K):
                piece = piece + gates[:, kk:kk + 1] * _load_slab_piece(cur.at[kk], s, tc)
            pieces.append(piece)
        xn = x_ref[...] + gt2_ref[0] * jnp.concatenate(pieces, axis=1)
        if final:
            xn = xn * lax.rsqrt(jnp.mean(xn * xn, axis=-1, keepdims=True) + NORM_EPS) * fg_ref[...]
        o_ref[...] = xn

        @pl.when(i == n_tiles - 1)
        def _():
            wait_tile(nxt, nxt_slot)

    @pl.when(i % 2 == 0)
    def _():
        step(buf_a, 0, buf_b, 1)

    @pl.when(i % 2 == 1)
    def _():
        step(buf_b, 1, buf_a, 0)


def _moe_combine(dest_flat, ys, x, gates, gt2, final_g, seq, final):
    t, d = x.shape
    tc = COMB_ROWS
    nb = seq // tc
    n_tiles = t // tc
    grid_spec = pltpu.PrefetchScalarGridSpec(
        num_scalar_prefetch=1,
        grid=(n_tiles,),
        in_specs=[
            pl.BlockSpec(memory_space=pl.ANY),
            pl.BlockSpec((tc, d), lambda i, ds: (i, 0)),
            pl.BlockSpec((tc, LANES), lambda i, ds: (i, 0)),
            pl.BlockSpec((1, 1, d), lambda i, ds: (i // nb, 0, 0)),
            pl.BlockSpec((1, d), lambda i, ds: (0, 0)),
        ],
        out_specs=pl.BlockSpec((tc, d), lambda i, ds: (i, 0)),
        scratch_shapes=[pltpu.VMEM((TOP_K, tc * SLAB_ROWS, SLAB_W), F32),
                        pltpu.VMEM((TOP_K, tc * SLAB_ROWS, SLAB_W), F32), pltpu.SemaphoreType.DMA((2,))],
    )
    return pl.pallas_call(
        functools.partial(_comb_kernel, final=final, n_tiles=n_tiles),
        grid_spec=grid_spec,
        out_shape=jax.ShapeDtypeStruct((t, d), F32),
        compiler_params=_cparams(("arbitrary",)),
        name="moe_combine",
    )(dest_flat, ys, x, gates, gt2, final_g)


def _pad_rows(w, n_rows, at=0):
    out = jnp.zeros((n_rows, w.shape[1]), w.dtype)
    return out.at[at:at + w.shape[0]].set(w)


def _layer_params(l, p):
    d = D_MODEL
    w_in = p["w_in"][l]
    sh, ret, gate = w_in[:, :N_SHIFT], w_in[:, N_SHIFT:N_SHIFT + N_RET], w_in[:, N_SHIFT + N_RET:]
    rkv = sh[:, :3 * RWKV_WIDTH]
    xw = sh[:, 3 * RWKV_WIDTH:3 * RWKV_WIDTH + LORA_DECAY]
    xa = sh[:, 3 * RWKV_WIDTH + LORA_DECAY:3 * RWKV_WIDTH + LORA_DECAY + LORA_A]
    xg = sh[:, 3 * RWKV_WIDTH + LORA_DECAY + LORA_A:]
    xg = jnp.pad(xg, ((0, 0), (0, XG_W - LORA_GATE)))
    mu = p["shift_mu"][l]
    mu_xg = jnp.pad(mu[3 * RWKV_WIDTH + LORA_DECAY + LORA_A:], (0, XG_W - LORA_GATE))
    mu_xv = jnp.concatenate([jnp.zeros((LORA_VRES,), F32), jnp.ones((LORA_VRES,), F32),
                             jnp.zeros((LANES - 2 * LORA_VRES,), F32)])
    if l == 0:
        xv = jnp.zeros((d, LANES), F32)
        v2 = jnp.zeros((LANES, RWKV_WIDTH), F32)
        v0 = jnp.zeros((RWKV_WIDTH,), F32)
    else:
        mu_v = p["vres_mu"][l - 1][:, None]
        v1 = p["vres_w1"][l - 1]
        xv = jnp.concatenate([v1 * (1.0 - mu_v), v1 * mu_v, jnp.zeros((d, LANES - 2 * LORA_VRES), F32)], axis=1)
        w2v = p["vres_w2"][l - 1]
        v2 = _pad_rows(jnp.concatenate([w2v, w2v], axis=0), LANES)
        v0 = p["vres_v0"][l - 1]
    w_proj = jnp.concatenate([ret, gate, rkv, xg, xw, xa, xv], axis=1).astype(BF16)
    mu_ext = jnp.concatenate([jnp.zeros((N_RET + 2 * d,), F32), mu[:3 * RWKV_WIDTH], mu_xg,
                              mu[3 * RWKV_WIDTH:3 * RWKV_WIDTH + LORA_DECAY + LORA_A], mu_xv])[None, :]
    row = lambda a: a.reshape(1, -1)
    pw = {
        "w0": row(p["rwkv_w0"][l]), "a0": row(p["rwkv_a0"][l]), "kks": row(p["rwkv_kk_scale"][l]),
        "ka": row(p["rwkv_k_a"][l]), "rk": row(p["rwkv_r_k"][l]), "lng": row(p["rwkv_lnx_g"][l]),
        "lnb": row(p["rwkv_lnx_b"][l]), "v0": row(v0),
        "w2": _pad_rows(p["rwkv_w2"][l], LANES, 0).astype(BF16),
        "a2": _pad_rows(p["rwkv_a2"][l], LANES, LORA_DECAY).astype(BF16),
        "g2": _pad_rows(p["rwkv_g2"][l], XG_W, 0).astype(BF16),
        "v2": v2.astype(BF16),
    }
    rw = jnp.pad(p["router_w"][l], ((0, 0), (0, LANES - N_EXPERTS)))
    rwh, rwl = _split_bf16(rw)
    rb = jnp.concatenate([p["router_b"][l], jnp.full((LANES - N_EXPERTS,), -1e30, F32)])[None, :]
    return {
        "w_proj": w_proj, "mu_ext": mu_ext, "pw": pw,
        "wa": p["w_up_a"][l].astype(BF16), "wb": p["w_up_b"][l].astype(BF16), "wo": p["w_out"][l].astype(BF16),
        "rwh": rwh, "rwl": rwl, "rb": rb,
        "n1g": row(p["norm1_g"][l]), "n2g": row(p["norm2_g"][l]), "gn_g": row(p["ret_gn_g"][l]),
    }


def _routing_tables(meta, counts, n_tok):
    blk = MOE_ROWS
    idx = meta[:, :TOP_K]
    rank = meta[:, TOP_K:2 * TOP_K]
    cnt = counts[0, :N_EXPERTS]
    pcnt = (cnt + blk - 1) // blk * blk
    pends = jnp.cumsum(pcnt)
    pstarts = pends - pcnt
    dest = (pstarts[idx] + rank).astype(I32)
    n_pad = n_tok * TOP_K + N_EXPERTS * blk
    nblk = n_pad // blk
    tok = jnp.broadcast_to(jnp.arange(n_tok, dtype=I32)[:, None], (n_tok, TOP_K))
    tok_tab = jnp.zeros((n_pad,), I32).at[dest.reshape(-1)].set(tok.reshape(-1), unique_indices=True)
    starts = jnp.arange(nblk, dtype=I32) * blk
    blk_e = jnp.minimum(jnp.sum((pends[None, :] <= starts[:, None]).astype(I32), axis=1), N_EXPERTS - 1)
    n_used = (pends[-1] // blk).astype(I32).reshape(1)
    return dest.reshape(-1), tok_tab, blk_e, n_used


def kernel(x, c, positions, ada_w, ada_b, norm1_g, norm2_g, w_in, shift_mu, rwkv_w0, rwkv_w2, rwkv_a0, rwkv_a2, rwkv_g2, rwkv_kk_scale, rwkv_k_a, rwkv_r_k, rwkv_lnx_g, rwkv_lnx_b, vres_mu, vres_w1, vres_w2, vres_v0, ret_gn_g, w_up_a, w_up_b, w_out, router_w, router_b, exp_w_gate, exp_b_gate, exp_w_up, exp_b_up, exp_w_down, exp_b_down, final_g):
    p = dict(w_in=w_in, shift_mu=shift_mu, rwkv_w0=rwkv_w0, rwkv_w2=rwkv_w2, rwkv_a0=rwkv_a0,
             rwkv_a2=rwkv_a2, rwkv_g2=rwkv_g2, rwkv_kk_scale=rwkv_kk_scale, rwkv_k_a=rwkv_k_a,
             rwkv_r_k=rwkv_r_k, rwkv_lnx_g=rwkv_lnx_g, rwkv_lnx_b=rwkv_lnx_b, vres_mu=vres_mu,
             vres_w1=vres_w1, vres_w2=vres_w2, vres_v0=vres_v0, ret_gn_g=ret_gn_g, w_up_a=w_up_a,
             w_up_b=w_up_b, w_out=w_out, router_w=router_w, router_b=router_b, norm1_g=norm1_g,
             norm2_g=norm2_g)
    batch, seq, d = x.shape
    n_layers = ada_w.shape[0]
    n_tok = batch * seq
    assert batch <= SUBLANES and d == D_MODEL

    c_pad = jnp.zeros((SUBLANES, d), F32).at[:batch].set(c)
    mod = _modulation(c_pad, ada_w, ada_b)[:, :batch].reshape(n_layers, batch, N_ADA, 1, d)

    half = RET_HEAD_DIM // 2
    inv_freq = ROPE_BASE ** (-jnp.arange(half, dtype=F32) / half)
    invf2 = jnp.concatenate([inv_freq, inv_freq])[None, :]
    cos2, sin2 = _rotary_tables(positions.reshape(n_tok, 1), invf2)
    log_gamma = jnp.log1p(-jnp.exp2(-5.0 - jnp.arange(RET_HEADS, dtype=F32)))
    log_gamma = jnp.broadcast_to(log_gamma[:, None, None], (RET_HEADS, 1, LANES))

    xf = x.reshape(n_tok, d)
    v_first = None
    for l in range(n_layers):
        lp = _layer_params(l, p)
        sh1, sc1, gt1, sh2, sc2, gt2 = (mod[l, :, i] for i in range(N_ADA))
        proj = _in_proj(xf, lp["n1g"], sc1, sh1, lp["w_proj"], seq)
        ya, v_first = _rwkv_mix(proj, lp["mu_ext"], lp["pw"], v_first, batch, seq)
        yb = _retention(proj, cos2, sin2, log_gamma, lp["gn_g"], batch, seq)
        xf, h2, logits = _merge_out(ya, yb, proj, xf, gt1, lp["n2g"], sc2, sh2, lp["wa"], lp["wb"],
                                    lp["wo"], lp["rwh"], lp["rwl"], lp["rb"], seq)
        meta, gates, counts = _route(logits)
        dest, tok_tab, blk_e, n_used = _routing_tables(meta, counts, n_tok)
        ys = _moe_experts(l, blk_e, n_used, tok_tab, h2, exp_w_gate, exp_b_gate, exp_w_up, exp_b_up,
                          exp_w_down, exp_b_down)
        xf = _moe_combine(dest, ys, xf, gates, gt2, final_g.reshape(1, d), seq, l == n_layers - 1)
    return xf.reshape(batch, seq, d)
```

```python
import functools
import math

import jax
import jax.numpy as jnp
from jax import lax
from jax.experimental import pallas as pl
from jax.experimental.pallas import tpu as pltpu

F32 = jnp.float32
BF16 = jnp.bfloat16
I32 = jnp.int32

D_MODEL = 2048
RWKV_HEAD_DIM = 64
RWKV_WIDTH = 1024
LORA_DECAY = 64
LORA_A = 64
LORA_VRES = 32
LORA_GATE = 160
RWKV_GN_EPS = 64e-5
RET_HEADS = 8
RET_HEAD_DIM = 128
RET_WIDTH = 1024
RET_GN_EPS = 1e-5
ROPE_BASE = 10000.0
N_EXPERTS = 32
TOP_K = 4
D_EXPERT = 512
SWIGLU_ALPHA = 1.702
SWIGLU_LIMIT = 7.0
NORM_EPS = 1e-6
N_ADA = 6
N_SHIFT = 3 * RWKV_WIDTH + LORA_DECAY + LORA_A + LORA_GATE
N_RET = 4 * RET_WIDTH

LANES = 128
SUBLANES = 8
MXU_DIM = 256

RET_OFF = 0
GATE_OFF = RET_OFF + N_RET
RW_OFF = GATE_OFF + 2 * D_MODEL
XG_OFF = RW_OFF + 3 * RWKV_WIDTH
XG_W = 256
XWA_OFF = XG_OFF + XG_W
XV_OFF = XWA_OFF + LANES
N_PROJ = XV_OFF + LANES

RWKV_CHUNK = 64
RWKV_ROWS = 256
RWKV_GROUPS_PER_BODY = 4
RET_ROWS = 256
MOE_ROWS = 256
MOE_WEIGHT_COPIES = 4
COMB_ROWS = 128
VMEM_LIMIT = 56 * 1024 * 1024


def _dot(a, b):
    return jnp.dot(a, b, preferred_element_type=F32)


def _dot_nt(a, b):
    return lax.dot_general(a, b, (((1,), (1,)), ((), ())), preferred_element_type=F32)


def _split_bf16(x):
    hi = x.astype(BF16)
    lo = (x - hi.astype(F32)).astype(BF16)
    return hi, lo


def _sigmoid(x):
    return 1.0 / (1.0 + jnp.exp(-x))


def _cparams(sem):
    return pltpu.CompilerParams(dimension_semantics=sem, vmem_limit_bytes=VMEM_LIMIT)


def _mod_kernel(c_ref, w_ref, b_ref, o_ref):
    c = c_ref[...]
    cs = c * _sigmoid(c)
    o_ref[0] = _dot(cs.astype(BF16), w_ref[0].astype(BF16)) + b_ref[0]


def _modulation(c_pad, ada_w, ada_b):
    n_layers, d, n = ada_w.shape
    tn = 1024
    return pl.pallas_call(
        _mod_kernel,
        grid=(n_layers, n // tn),
        in_specs=[
            pl.BlockSpec((SUBLANES, d), lambda l, j: (0, 0)),
            pl.BlockSpec((1, d, tn), lambda l, j: (l, 0, j)),
            pl.BlockSpec((1, 1, tn), lambda l, j: (l, 0, j)),
        ],
        out_specs=pl.BlockSpec((1, SUBLANES, tn), lambda l, j: (l, 0, j)),
        out_shape=jax.ShapeDtypeStruct((n_layers, SUBLANES, n), F32),
        compiler_params=_cparams(("parallel", "parallel")),
        name="adaln_mod",
    )(c_pad, ada_w, ada_b.reshape(n_layers, 1, n))


def _trig_kernel(pos_ref, invf_ref, cos_ref, sin_ref):
    ang = pos_ref[...].astype(F32) * invf_ref[...]
    lane = lax.broadcasted_iota(I32, ang.shape, 1)
    s = jnp.sin(ang)
    cos_ref[...] = jnp.cos(ang)
    sin_ref[...] = jnp.where(lane < RET_HEAD_DIM // 2, -s, s)


def _rotary_tables(pos_col, invf2):
    t = pos_col.shape[0]
    tm = 1024
    return pl.pallas_call(
        _trig_kernel,
        grid=(t // tm,),
        in_specs=[pl.BlockSpec((tm, 1), lambda i: (i, 0)), pl.BlockSpec((1, LANES), lambda i: (0, 0))],
        out_specs=[pl.BlockSpec((tm, LANES), lambda i: (i, 0))] * 2,
        out_shape=[jax.ShapeDtypeStruct((t, LANES), F32)] * 2,
        compiler_params=_cparams(("parallel",)),
        name="rotary_tables",
    )(pos_col, invf2)


def _in_kernel(x_ref, g_ref, sc_ref, sh_ref, w_ref, o_ref, h_scr):
    @pl.when(pl.program_id(1) == 0)
    def _():
        x = x_ref[...]
        y = x * lax.rsqrt(jnp.mean(x * x, axis=-1, keepdims=True) + NORM_EPS)
        h_scr[...] = (y * g_ref[...] * (1.0 + sc_ref[0]) + sh_ref[0]).astype(BF16)

    o_ref[...] = _dot(h_scr[...], w_ref[...])


def _in_proj(x, g, sc, sh, w, seq):
    t, d = x.shape
    n = w.shape[1]
    tm = min(1024, seq)
    tn = 512
    return pl.pallas_call(
        _in_kernel,
        grid=(t // tm, n // tn),
        in_specs=[
            pl.BlockSpec((tm, d), lambda i, j: (i, 0)),
            pl.BlockSpec((1, d), lambda i, j: (0, 0)),
            pl.BlockSpec((1, 1, d), lambda i, j: (i * tm // seq, 0, 0)),
            pl.BlockSpec((1, 1, d), lambda i, j: (i * tm // seq, 0, 0)),
            pl.BlockSpec((d, tn), lambda i, j: (0, j)),
        ],
        out_specs=pl.BlockSpec((tm, tn), lambda i, j: (i, j)),
        out_shape=jax.ShapeDtypeStruct((t, n), F32),
        scratch_shapes=[pltpu.VMEM((tm, d), BF16)],
        compiler_params=_cparams(("parallel", "arbitrary")),
        name="norm_in_proj",
    )(x, g, sc, sh, w)


def _bdot(a, b):
    return jnp.einsum("bij,bjk->bik", a, b, preferred_element_type=F32)


def _bdot_nt(a, b):
    return jnp.einsum("bik,bjk->bij", a, b, preferred_element_type=F32)


def _rwkv_chunk_terms(at, rt, bt, kt, bh, kh, v, wc_row):
    c = RWKV_CHUNK
    n = RWKV_HEAD_DIM
    lw = LANES
    nb = at.shape[0]
    reps = lw // c
    row = lax.broadcasted_iota(I32, (nb, lw, lw), 1)
    col = lax.broadcasted_iota(I32, (nb, lw, lw), 2)
    same_head = (row // n) == (col // n)

    def stack(x):
        return jnp.where(same_head, jnp.concatenate([x] * reps, axis=1), 0.0)

    def transpose(x):
        return jnp.stack([x[i].T for i in range(nb)])

    at2 = stack(at)
    rt2 = stack(rt)
    v2 = stack(v).astype(BF16)
    lhs = jnp.concatenate([at2, rt2], axis=1).astype(BF16)
    rhs = jnp.concatenate([stack(bt), stack(kt)], axis=1).astype(BF16)
    aa = _bdot_nt(lhs, rhs)
    strict = row > col
    lower = row >= col
    a_ab = jnp.where(strict, aa[:, :lw, :lw], 0.0)
    a_ak = jnp.where(strict, aa[:, :lw, lw:], 0.0)
    a_rb = jnp.where(lower, aa[:, lw:, :lw], 0.0)
    a_rk = jnp.where(lower, aa[:, lw:, lw:], 0.0)

    x = jnp.where(row == col, 1.0, 0.0) + jnp.where(((row ^ col) == 1) & ((row & 1) == 1), a_ab, 0.0)
    b = 2
    while b < c:
        sel = ((row // (2 * b)) == (col // (2 * b))) & ((row & b) != 0) & ((col & b) == 0)
        xb = x.astype(BF16)
        y = _bdot(xb, jnp.where(sel, a_ab, 0.0).astype(BF16))
        x = x + _bdot(y.astype(BF16), xb)
        b *= 2

    akv = _bdot(a_ak.astype(BF16), v2)
    p12b = _bdot(x.astype(BF16), jnp.concatenate([at2, akv], axis=2).astype(BF16)).astype(BF16)
    arb = _bdot(a_rb.astype(BF16), p12b)
    qp = (rt2 + arb[:, :, :lw]).astype(BF16)
    o_in = arb[:, :, lw:] + _bdot(a_rk.astype(BF16), v2)
    bp = _bdot(transpose(stack(bh)).astype(BF16), p12b)
    ht = bp[:, :, lw:] + _bdot(transpose(stack(kh)).astype(BF16), v2)
    wc_col = jnp.sum(jnp.where(row == col, wc_row, 0.0), axis=2, keepdims=True)
    return qp, o_in, bp[:, :, :lw].astype(BF16), ht, wc_col


def _rwkv_kernel(*refs, rows, has_vres):
    c = RWKV_CHUNK
    n = RWKV_HEAD_DIM
    lw = LANES
    ng = RWKV_WIDTH // lw
    nc = rows // c
    gpb = RWKV_GROUPS_PER_BODY
    it = iter(refs)
    r_ref, k_ref, v_ref, xg_ref, xwa_ref, xv_ref = (next(it) for _ in range(6))
    mu_r, mu_k, mu_v, mu_xg, mu_xwa, mu_xv = (next(it) for _ in range(6))
    w0, a0, kks, ka, rk, lng, lnb, v0 = (next(it) for _ in range(8))
    w2_ref, a2_ref, g2_ref, v2_ref = (next(it) for _ in range(4))
    vf_ref = next(it) if has_vres else None
    ya_ref = next(it)
    vfo_ref = None if has_vres else next(it)
    c_r, c_k, c_v, c_xg, c_xwa, c_xv = (next(it) for _ in range(6))
    st_s = next(it)
    at_s, rt_s, bt_s, kt_s, bh_s, kh_s, vv_s, wc_s, o_s = (next(it) for _ in range(9))
    bonus_s, gate_s = next(it), next(it)

    @pl.when(pl.program_id(1) == 0)
    def _():
        for cr in (c_r, c_k, c_v, c_xg, c_xwa, c_xv):
            cr[...] = jnp.zeros_like(cr)
        st_s[...] = jnp.zeros_like(st_s)

    def shift(p_ref, mu_ref, carry_ref):
        p = p_ref[...]
        row = lax.broadcasted_iota(I32, p.shape, 0)
        prev = jnp.where(row == 0, carry_ref[0:1, :], pltpu.roll(p, 1, 0))
        carry_ref[0:1, :] = p[rows - 1:rows, :]
        return p + (prev - p) * mu_ref[...]

    r = shift(r_ref, mu_r, c_r)
    k = shift(k_ref, mu_k, c_k)
    v = shift(v_ref, mu_v, c_v)
    xg = shift(xg_ref, mu_xg, c_xg)
    xwa = shift(xwa_ref, mu_xwa, c_xwa)

    z = w0[...] + _dot(jnp.tanh(xwa).astype(BF16), w2_ref[...])
    lw_log = -math.exp(-0.5) * _sigmoid(z)
    a = _sigmoid(a0[...] + _dot(xwa.astype(BF16), a2_ref[...]))
    gate_s[...] = _dot(_sigmoid(xg).astype(BF16), g2_ref[...])
    if has_vres:
        xv = shift(xv_ref, mu_xv, c_xv)
        v = v + (vf_ref[...] - v) * _sigmoid(v0[...] + _dot(xv.astype(BF16), v2_ref[...]))
    else:
        vfo_ref[...] = v

    hrow = lax.broadcasted_iota(I32, (MXU_DIM, MXU_DIM), 0) // n
    hcol = lax.broadcasted_iota(I32, (MXU_DIM, MXU_DIM), 1) // n
    head_ones = (hrow == hcol).astype(BF16)

    def headsum(x):
        xb = x.astype(BF16)
        return jnp.concatenate([_dot(xb[:, i * MXU_DIM:(i + 1) * MXU_DIM], head_ones)
                                for i in range(RWKV_WIDTH // MXU_DIM)], axis=1)

    kk0 = k * kks[...]
    kk = kk0 * lax.rsqrt(jnp.maximum(headsum(kk0 * kk0), 1e-24))
    km = k * (1.0 + (a - 1.0) * ka[...])
    bonus_s[...] = headsum(r * km * rk[...]) * v

    trow = lax.broadcasted_iota(I32, (rows, rows), 0) // c
    tcol = lax.broadcasted_iota(I32, (rows, rows), 1) // c
    same_chunk = trow == tcol
    tri = (same_chunk & (lax.broadcasted_iota(I32, (rows, rows), 1)
                         <= lax.broadcasted_iota(I32, (rows, rows), 0))).astype(BF16)
    blk_ones = same_chunk.astype(BF16)
    lhi, llo = _split_bf16(lw_log)
    cs = _dot(tri, lhi) + _dot(tri, llo)
    ctot = _dot(blk_ones, lhi) + _dot(blk_ones, llo)
    b_vec = kk * a
    w_inv = jnp.exp(-cs)
    tail = jnp.exp(ctot - cs)
    planes = ((at_s, -kk * jnp.exp(cs - lw_log)), (rt_s, r * jnp.exp(cs)), (bt_s, b_vec * w_inv),
              (kt_s, km * w_inv), (bh_s, b_vec * tail), (kh_s, km * tail), (vv_s, v), (wc_s, jnp.exp(ctot)))
    for dst, val in planes:
        for g in range(ng):
            dst[g] = val[:, g * lw:(g + 1) * lw]

    def pair_body(gp, carry):
        gs = pl.ds(gp * gpb, gpb)

        def chunks(ref):
            return ref[gs].reshape(gpb * nc, c, lw)

        qp_b, o_in_b, bpg_b, ht_b, wc_col_b = _rwkv_chunk_terms(
            chunks(at_s), chunks(rt_s), chunks(bt_s), chunks(kt_s), chunks(bh_s), chunks(kh_s),
            chunks(vv_s), chunks(wc_s)[:, 0:1, :])
        states = [st_s[gp * gpb + gg] for gg in range(gpb)]
        for ci in range(nc):
            for gg in range(gpb):
                bi = gg * nc + ci
                qp, o_in, bpg, ht, wc_col = qp_b[bi], o_in_b[bi], bpg_b[bi], ht_b[bi], wc_col_b[bi]
                st = states[gg]
                stb = st.astype(BF16)
                o2 = _dot(qp, stb) + o_in
                o = o2[0:c, :]
                for h in range(1, lw // c):
                    o = o + o2[h * c:(h + 1) * c, :]
                o_s[gp * gpb + gg, pl.ds(ci * c, c), :] = o
                states[gg] = wc_col * st + _dot(bpg, stb) + ht
        for gg in range(gpb):
            st_s[gp * gpb + gg] = states[gg]
        return carry

    lax.fori_loop(0, ng // gpb, pair_body, 0)

    o = jnp.concatenate([o_s[g] for g in range(ng)], axis=1)
    mean = headsum(o) * (1.0 / n)
    dev = o - mean
    var = headsum(dev * dev) * (1.0 / n)
    on = dev * lax.rsqrt(var + RWKV_GN_EPS) * lng[...] + lnb[...]
    ya_ref[...] = ((on + bonus_s[...]) * gate_s[...]).astype(ya_ref.dtype)


def _rwkv_mix(proj, mu_ext, pw, v_first, batch, seq):
    t = proj.shape[0]
    rows = min(RWKV_ROWS, seq)
    w = RWKV_WIDTH
    nj = seq // rows
    ng = w // LANES
    has_vres = v_first is not None

    def colspec(width, off):
        return pl.BlockSpec((rows, width), lambda b, j: (b * nj + j, off // width))

    def muspec(width, off):
        return pl.BlockSpec((1, width), lambda b, j: (0, off // width))

    layout = [(w, RW_OFF), (w, RW_OFF + w), (w, RW_OFF + 2 * w), (XG_W, XG_OFF), (LANES, XWA_OFF), (LANES, XV_OFF)]
    in_specs = [colspec(*a) for a in layout] + [muspec(*a) for a in layout]
    args = [proj] * 6 + [mu_ext] * 6
    for name in ("w0", "a0", "kks", "ka", "rk", "lng", "lnb", "v0"):
        in_specs.append(pl.BlockSpec((1, w), lambda b, j: (0, 0)))
        args.append(pw[name])
    for name, kdim in (("w2", LANES), ("a2", LANES), ("g2", XG_W), ("v2", LANES)):
        in_specs.append(pl.BlockSpec((kdim, w), lambda b, j: (0, 0)))
        args.append(pw[name])
    io_spec = pl.BlockSpec((rows, w), lambda b, j: (b * nj + j, 0))
    out_specs = [io_spec]
    out_shape = [jax.ShapeDtypeStruct((t, w), BF16)]
    if has_vres:
        in_specs.append(io_spec)
        args.append(v_first)
    else:
        out_specs.append(io_spec)
        out_shape.append(jax.ShapeDtypeStruct((t, w), F32))
    scratch = [pltpu.VMEM((SUBLANES, w), F32)] * 3 + [pltpu.VMEM((SUBLANES, XG_W), F32)]
    scratch += [pltpu.VMEM((SUBLANES, LANES), F32)] * 2
    scratch += [pltpu.VMEM((ng, LANES, LANES), F32)]
    scratch += [pltpu.VMEM((ng, rows, LANES), F32)] * 9
    scratch += [pltpu.VMEM((rows, w), F32)] * 2
    outs = pl.pallas_call(
        functools.partial(_rwkv_kernel, rows=rows, has_vres=has_vres),
        grid=(batch, nj),
        in_specs=in_specs,
        out_specs=out_specs,
        out_shape=out_shape,
        scratch_shapes=scratch,
        compiler_params=_cparams(("parallel", "arbitrary")),
        name="rwkv7_mix",
    )(*args)
    if has_vres:
        return outs[0], v_first
    return outs[0], outs[1]


def _ret_kernel(q_ref, k_ref, v_ref, g_ref, cos_ref, sin_ref, lg_ref, gn_ref, o_ref, st_ref, *, rows):
    @pl.when(pl.program_id(1) == 0)
    def _():
        st_ref[...] = jnp.zeros_like(st_ref)

    hd = RET_HEAD_DIM
    cos = cos_ref[...]
    sin = sin_ref[...]
    row = lax.broadcasted_iota(I32, (rows, rows), 0)
    col = lax.broadcasted_iota(I32, (rows, rows), 1)
    causal = row >= col
    rel = jnp.maximum(row - col, 0).astype(F32)
    idx = lax.broadcasted_iota(I32, (rows, LANES), 0).astype(F32)

    def rot(x):
        return x * cos + pltpu.roll(x, hd // 2, 1) * sin

    for h in range(RET_HEADS):
        hs = slice(h * hd, (h + 1) * hd)
        lg = lg_ref[h]
        q = rot(q_ref[:, hs])
        k = rot(k_ref[:, hs]) * (hd ** -0.5)
        v = v_ref[:, hs].astype(BF16)
        lg_sq = jnp.concatenate([lg] * (rows // LANES), axis=1) if rows > LANES else lg
        decay = jnp.where(causal, jnp.exp(rel * lg_sq), 0.0)
        qb = q.astype(BF16)
        scores = _dot_nt(qb, k.astype(BF16)) * decay
        intra = _dot(scores.astype(BF16), v)
        xi = jnp.exp((idx + 1.0) * lg)
        st = st_ref[h]
        cross = _dot(qb, st.astype(BF16)) * xi
        zeta = jnp.exp((rows - 1.0 - idx) * lg)
        kz = (k * zeta).T.astype(BF16)
        chunk_decay = jnp.exp(rows * lg)
        st_ref[h] = st * chunk_decay + _dot(kz, v)
        o = intra + cross
        mean = jnp.mean(o, axis=-1, keepdims=True)
        dev = o - mean
        var = jnp.mean(dev * dev, axis=-1, keepdims=True)
        on = dev * lax.rsqrt(var + RET_GN_EPS) * gn_ref[:, hs]
        gr = g_ref[:, hs]
        o_ref[:, hs] = (gr * _sigmoid(gr) * on).astype(o_ref.dtype)


def _retention(proj, cos2, sin2, log_gamma, gn_g, batch, seq):
    t = proj.shape[0]
    rows = min(RET_ROWS, seq)
    nj = seq // rows
    w = RET_WIDTH

    def pspec(off):
        return pl.BlockSpec((rows, w), lambda b, j: (b * nj + j, off // w))

    tspec = pl.BlockSpec((rows, LANES), lambda b, j: (b * nj + j, 0))
    return pl.pallas_call(
        functools.partial(_ret_kernel, rows=rows),
        grid=(batch, nj),
        in_specs=[pspec(RET_OFF), pspec(RET_OFF + w), pspec(RET_OFF + 2 * w), pspec(RET_OFF + 3 * w),
                  tspec, tspec,
                  pl.BlockSpec((RET_HEADS, 1, LANES), lambda b, j: (0, 0, 0)),
                  pl.BlockSpec((1, w), lambda b, j: (0, 0))],
        out_specs=pl.BlockSpec((rows, w), lambda b, j: (b * nj + j, 0)),
        out_shape=jax.ShapeDtypeStruct((t, w), BF16),
        scratch_shapes=[pltpu.VMEM((RET_HEADS, RET_HEAD_DIM, RET_HEAD_DIM), F32)],
        compiler_params=_cparams(("parallel", "arbitrary")),
        name="retention",
    )(proj, proj, proj, proj, cos2, sin2, log_gamma, gn_g)


def _out_kernel(ya_ref, yb_ref, ga_ref, gb_ref, x_ref, gt1_ref, n2g_ref, sc2_ref, sh2_ref,
                wa_ref, wb_ref, wo_ref, rwh_ref, rwl_ref, rb_ref, xo_ref, h2_ref, lg_ref):
    ua = _dot(ya_ref[...], wa_ref[...])
    ub = _dot(yb_ref[...], wb_ref[...])
    y = _sigmoid(ga_ref[...]) * ua + _sigmoid(gb_ref[...]) * ub
    xn = x_ref[...] + gt1_ref[0] * _dot(y.astype(BF16), wo_ref[...])
    xo_ref[...] = xn
    yn = xn * lax.rsqrt(jnp.mean(xn * xn, axis=-1, keepdims=True) + NORM_EPS)
    h2 = yn * n2g_ref[...] * (1.0 + sc2_ref[0]) + sh2_ref[0]
    h2_ref[...] = h2
    hi, lo = _split_bf16(h2)
    lg_ref[...] = _dot(hi, rwh_ref[...]) + _dot(hi, rwl_ref[...]) + _dot(lo, rwh_ref[...]) + rb_ref[...]


def _merge_out(ya, yb, proj, x, gt1, n2g, sc2, sh2, wa, wb, wo, rwh, rwl, rb, seq):
    t, d = x.shape
    tm = 256
    nb = seq // tm
    const = lambda i: (0, 0)
    mod = pl.BlockSpec((1, 1, d), lambda i: (i // nb, 0, 0))
    single = pl.Buffered(1)
    return pl.pallas_call(
        _out_kernel,
        grid=(t // tm,),
        in_specs=[
            pl.BlockSpec((tm, RWKV_WIDTH), lambda i: (i, 0)),
            pl.BlockSpec((tm, RET_WIDTH), lambda i: (i, 0)),
            pl.BlockSpec((tm, d), lambda i: (i, GATE_OFF // d)),
            pl.BlockSpec((tm, d), lambda i: (i, GATE_OFF // d + 1)),
            pl.BlockSpec((tm, d), lambda i: (i, 0)),
            mod, pl.BlockSpec((1, d), const), mod, mod,
            pl.BlockSpec((RWKV_WIDTH, d), const, pipeline_mode=single),
            pl.BlockSpec((RET_WIDTH, d), const, pipeline_mode=single),
            pl.BlockSpec((d, d), const, pipeline_mode=single),
            pl.BlockSpec((d, LANES), const, pipeline_mode=single),
            pl.BlockSpec((d, LANES), const, pipeline_mode=single),
            pl.BlockSpec((1, LANES), const),
        ],
        out_specs=[pl.BlockSpec((tm, d), lambda i: (i, 0)), pl.BlockSpec((tm, d), lambda i: (i, 0)),
                   pl.BlockSpec((tm, LANES), lambda i: (i, 0))],
        out_shape=[jax.ShapeDtypeStruct((t, d), F32), jax.ShapeDtypeStruct((t, d), F32),
                   jax.ShapeDtypeStruct((t, LANES), F32)],
        compiler_params=_cparams(("parallel",)),
        name="merge_out_norm_router",
    )(ya, yb, proj, proj, x, gt1, n2g, sc2, sh2, wa, wb, wo, rwh, rwl, rb)


def _route_kernel(lg_ref, meta_ref, gate_ref, cnt_ref, carry_ref, *, tm):
    @pl.when(pl.program_id(0) == 0)
    def _():
        carry_ref[...] = jnp.zeros_like(carry_ref)

    l = lg_ref[...]
    lane = lax.broadcasted_iota(I32, l.shape, 1)
    lane_f = lane.astype(F32)
    vals, idxs = [], []
    for _ in range(TOP_K):
        m = jnp.max(l, axis=-1, keepdims=True)
        idx = jnp.min(jnp.where(l == m, lane_f, float(LANES)), axis=-1, keepdims=True)
        vals.append(m)
        idxs.append(idx)
        l = jnp.where(lane_f == idx, -jnp.inf, l)
    es = [jnp.exp(vv - vals[0]) for vv in vals]
    denom = es[0] + es[1] + es[2] + es[3]
    sel = jnp.zeros(l.shape, F32)
    for idx in idxs:
        sel = sel + jnp.where(lane_f == idx, 1.0, 0.0)
    row = lax.broadcasted_iota(I32, (tm, tm), 0)
    col = lax.broadcasted_iota(I32, (tm, tm), 1)
    tri = (col <= row).astype(BF16)
    cum = _dot(tri, sel.astype(BF16)) + carry_ref[0:1, :]
    excl = cum - sel
    carry_ref[0:1, :] = cum[tm - 1:tm, :]
    meta = jnp.zeros(l.shape, F32)
    gates = jnp.zeros(l.shape, F32)
    for kk in range(TOP_K):
        rank = jnp.sum(jnp.where(lane_f == idxs[kk], excl, 0.0), axis=-1, keepdims=True)
        meta = meta + jnp.where(lane == kk, idxs[kk], 0.0) + jnp.where(lane == TOP_K + kk, rank, 0.0)
        gates = gates + jnp.where(lane == kk, es[kk] / denom, 0.0)
    meta_ref[...] = meta.astype(I32)
    gate_ref[...] = gates
    cnt_ref[...] = jnp.broadcast_to(cum[tm - 1:tm, :], cnt_ref.shape).astype(I32)


def _route(logits):
    t = logits.shape[0]
    tm = 256
    return pl.pallas_call(
        functools.partial(_route_kernel, tm=tm),
        grid=(t // tm,),
        in_specs=[pl.BlockSpec((tm, LANES), lambda i: (i, 0))],
        out_specs=[pl.BlockSpec((tm, LANES), lambda i: (i, 0)), pl.BlockSpec((tm, LANES), lambda i: (i, 0)),
                   pl.BlockSpec((SUBLANES, LANES), lambda i: (0, 0))],
        out_shape=[jax.ShapeDtypeStruct((t, LANES), I32), jax.ShapeDtypeStruct((t, LANES), F32),
                   jax.ShapeDtypeStruct((SUBLANES, LANES), I32)],
        scratch_shapes=[pltpu.VMEM((SUBLANES, LANES), F32)],
        compiler_params=_cparams(("arbitrary",)),
        name="router_topk",
    )(logits)


def _moe_kernel(be_ref, nx_ref, ws_ref, nu_ref, tok_ref, h_hbm, wg_hbm, wu_hbm, wd_hbm, bg_ref, bu_ref, bd_ref,
                ys_ref, xbuf, xb_s, wg_buf, wu_buf, wd_buf, wgu_s, wd_s, sem, wsem, *, layer):
    i = pl.program_id(0)
    n_used = nu_ref[0]
    blk = MOE_ROWS
    f = D_EXPERT
    e = be_ref[i]
    wslot = ws_ref[i]

    def row_copy(tok, r, slot):
        return pltpu.make_async_copy(h_hbm.at[pl.ds(tok, 1), :], xbuf.at[slot, pl.ds(r, 1), :], sem.at[slot])

    def wait_block(slot):
        pltpu.make_async_copy(h_hbm.at[pl.ds(0, blk), :], xbuf.at[slot], sem.at[slot]).wait()

    def weight_copies(expert, slot):
        copies = []
        for src, dst in ((wg_hbm, wg_buf), (wu_hbm, wu_buf), (wd_hbm, wd_buf)):
            rows = src.shape[2] // MOE_WEIGHT_COPIES
            for part in range(MOE_WEIGHT_COPIES):
                rs = pl.ds(part * rows, rows)
                copies.append(pltpu.make_async_copy(src.at[layer, expert, rs, :], dst.at[slot, rs, :],
                                                    wsem.at[slot]))
        return copies

    @pl.when(i == 0)
    def _():
        for cp in weight_copies(e, wslot):
            cp.start()

        def body(r, carry):
            row_copy(tok_ref[r], r, 0).start()
            return carry
        lax.fori_loop(0, blk, body, 0)

    prev_e = be_ref[jnp.maximum(i - 1, 0)]

    @pl.when((i < n_used) & ((i == 0) | (e != prev_e)))
    def _():
        for cp in weight_copies(e, wslot):
            cp.wait()
        wgu_s[:, :f] = wg_buf[wslot].astype(BF16)
        wgu_s[:, f:] = wu_buf[wslot].astype(BF16)
        wd_s[...] = wd_buf[wslot].astype(BF16)
        nxt = nx_ref[i]

        @pl.when(nxt >= 0)
        def _():
            for cp in weight_copies(nxt, 1 - wslot):
                cp.start()

    @pl.when(i < n_used)
    def _():
        slot = i % 2
        wait_block(slot)
        xb_s[...] = xbuf[slot].astype(BF16)
        base = (i + 1) * blk
        for r in range(blk):
            row_copy(tok_ref[base + r], r, 1 - slot).start()
        x = xb_s[...]
        gt = jnp.minimum(_dot(x, wgu_s[:, :f]) + bg_ref[0, 0], SWIGLU_LIMIT)
        up = jnp.clip(_dot(x, wgu_s[:, f:]) + bu_ref[0, 0], -SWIGLU_LIMIT, SWIGLU_LIMIT)
        act = (up + 1.0) * (gt * _sigmoid(gt * SWIGLU_ALPHA))
        ys_ref[...] = _dot(act.astype(BF16), wd_s[...]) + bd_ref[0, 0]

    @pl.when(i == n_used)
    def _():
        wait_block(i % 2)

    @pl.when(i >= n_used)
    def _():
        ys_ref[...] = jnp.zeros_like(ys_ref)


def _moe_experts(layer, blk_e, nxt_e, w_slot, n_used, tok_tab, h2, w_gate, b_gate, w_up, b_up, w_down, b_down):
    d = D_MODEL
    f = D_EXPERT
    n_pad = tok_tab.shape[0]
    blk = MOE_ROWS
    nblk = n_pad // blk
    bmap = lambda i, be, nx, ws, nu, tk: (layer, be[i], 0, 0)
    grid_spec = pltpu.PrefetchScalarGridSpec(
        num_scalar_prefetch=5,
        grid=(nblk,),
        in_specs=[
            pl.BlockSpec(memory_space=pl.ANY),
            pl.BlockSpec(memory_space=pl.ANY),
            pl.BlockSpec(memory_space=pl.ANY),
            pl.BlockSpec(memory_space=pl.ANY),
            pl.BlockSpec((1, 1, 1, f), bmap),
            pl.BlockSpec((1, 1, 1, f), bmap),
            pl.BlockSpec((1, 1, 1, d), bmap),
        ],
        out_specs=pl.BlockSpec((blk, d), lambda i, be, nx, ws, nu, tk: (i, 0)),
        scratch_shapes=[pltpu.VMEM((2, blk, d), F32), pltpu.VMEM((blk, d), BF16),
                        pltpu.VMEM((2, d, f), F32), pltpu.VMEM((2, d, f), F32), pltpu.VMEM((2, f, d), F32),
                        pltpu.VMEM((d, 2 * f), BF16), pltpu.VMEM((f, d), BF16),
                        pltpu.SemaphoreType.DMA((2,)), pltpu.SemaphoreType.DMA((2,))],
    )
    n_layers, n_exp = w_gate.shape[:2]
    return pl.pallas_call(
        functools.partial(_moe_kernel, layer=layer),
        grid_spec=grid_spec,
        out_shape=jax.ShapeDtypeStruct((n_pad, d), F32),
        compiler_params=_cparams(("arbitrary",)),
        name="moe_experts",
    )(blk_e, nxt_e, w_slot, n_used, tok_tab, h2, w_gate, w_up, w_down, b_gate.reshape(n_layers, n_exp, 1, f),
      b_up.reshape(n_layers, n_exp, 1, f), b_down.reshape(n_layers, n_exp, 1, d))


def _comb_kernel(dest_ref, ys_hbm, x_ref, gate_ref, gt2_ref, fg_ref, o_ref, buf, sem, *, final, n_tiles):
    i = pl.program_id(0)
    tc = COMB_ROWS

    def row_copy(base, r, kk, slot):
        return pltpu.make_async_copy(ys_hbm.at[pl.ds(dest_ref[base + r * TOP_K + kk], 1), :],
                                     buf.at[slot, kk, pl.ds(r, 1), :], sem.at[slot])

    def wait_tile(slot):
        for kk in range(TOP_K):
            pltpu.make_async_copy(ys_hbm.at[pl.ds(0, tc), :], buf.at[slot, kk], sem.at[slot]).wait()

    @pl.when(i == 0)
    def _():
        def body(r, carry):
            for kk in range(TOP_K):
                row_copy(0, r, kk, 0).start()
            return carry
        lax.fori_loop(0, tc, body, 0)

    slot = i % 2
    wait_tile(slot)
    base = jnp.minimum(i + 1, n_tiles - 1) * (tc * TOP_K)
    for r in range(tc):
        for kk in range(TOP_K):
            row_copy(base, r, kk, 1 - slot).start()
    gates = gate_ref[...]
    moe = gates[:, 0:1] * buf[slot, 0]
    for kk in range(1, TOP_K):
        moe = moe + gates[:, kk:kk + 1] * buf[slot, kk]
    xn = x_ref[...] + gt2_ref[0] * moe
    if final:
        xn = xn * lax.rsqrt(jnp.mean(xn * xn, axis=-1, keepdims=True) + NORM_EPS) * fg_ref[...]
    o_ref[...] = xn

    @pl.when(i == n_tiles - 1)
    def _():
        wait_tile(1 - slot)


def _moe_combine(dest_flat, ys, x, gates, gt2, final_g, seq, final):
    t, d = x.shape
    tc = COMB_ROWS
    nb = seq // tc
    n_tiles = t // tc
    grid_spec = pltpu.PrefetchScalarGridSpec(
        num_scalar_prefetch=1,
        grid=(n_tiles,),
        in_specs=[
            pl.BlockSpec(memory_space=pl.ANY),
            pl.BlockSpec((tc, d), lambda i, ds: (i, 0)),
            pl.BlockSpec((tc, LANES), lambda i, ds: (i, 0)),
            pl.BlockSpec((1, 1, d), lambda i, ds: (i // nb, 0, 0)),
            pl.BlockSpec((1, d), lambda i, ds: (0, 0)),
        ],
        out_specs=pl.BlockSpec((tc, d), lambda i, ds: (i, 0)),
        scratch_shapes=[pltpu.VMEM((2, TOP_K, tc, d), F32), pltpu.SemaphoreType.DMA((2,))],
    )
    return pl.pallas_call(
        functools.partial(_comb_kernel, final=final, n_tiles=n_tiles),
        grid_spec=grid_spec,
        out_shape=jax.ShapeDtypeStruct((t, d), F32),
        compiler_params=_cparams(("arbitrary",)),
        name="moe_combine",
    )(dest_flat, ys, x, gates, gt2, final_g)


def _pad_rows(w, n_rows, at=0):
    out = jnp.zeros((n_rows, w.shape[1]), w.dtype)
    return out.at[at:at + w.shape[0]].set(w)


def _layer_params(l, p):
    d = D_MODEL
    w_in = p["w_in"][l]
    sh, ret, gate = w_in[:, :N_SHIFT], w_in[:, N_SHIFT:N_SHIFT + N_RET], w_in[:, N_SHIFT + N_RET:]
    rkv = sh[:, :3 * RWKV_WIDTH]
    xw = sh[:, 3 * RWKV_WIDTH:3 * RWKV_WIDTH + LORA_DECAY]
    xa = sh[:, 3 * RWKV_WIDTH + LORA_DECAY:3 * RWKV_WIDTH + LORA_DECAY + LORA_A]
    xg = sh[:, 3 * RWKV_WIDTH + LORA_DECAY + LORA_A:]
    xg = jnp.pad(xg, ((0, 0), (0, XG_W - LORA_GATE)))
    mu = p["shift_mu"][l]
    mu_xg = jnp.pad(mu[3 * RWKV_WIDTH + LORA_DECAY + LORA_A:], (0, XG_W - LORA_GATE))
    mu_xv = jnp.concatenate([jnp.zeros((LORA_VRES,), F32), jnp.ones((LORA_VRES,), F32),
                             jnp.zeros((LANES - 2 * LORA_VRES,), F32)])
    if l == 0:
        xv = jnp.zeros((d, LANES), F32)
        v2 = jnp.zeros((LANES, RWKV_WIDTH), F32)
        v0 = jnp.zeros((RWKV_WIDTH,), F32)
    else:
        mu_v = p["vres_mu"][l - 1][:, None]
        v1 = p["vres_w1"][l - 1]
        xv = jnp.concatenate([v1 * (1.0 - mu_v), v1 * mu_v, jnp.zeros((d, LANES - 2 * LORA_VRES), F32)], axis=1)
        w2v = p["vres_w2"][l - 1]
        v2 = _pad_rows(jnp.concatenate([w2v, w2v], axis=0), LANES)
        v0 = p["vres_v0"][l - 1]
    w_proj = jnp.concatenate([ret, gate, rkv, xg, xw, xa, xv], axis=1).astype(BF16)
    mu_ext = jnp.concatenate([jnp.zeros((N_RET + 2 * d,), F32), mu[:3 * RWKV_WIDTH], mu_xg,
                              mu[3 * RWKV_WIDTH:3 * RWKV_WIDTH + LORA_DECAY + LORA_A], mu_xv])[None, :]
    row = lambda a: a.reshape(1, -1)
    pw = {
        "w0": row(p["rwkv_w0"][l]), "a0": row(p["rwkv_a0"][l]), "kks": row(p["rwkv_kk_scale"][l]),
        "ka": row(p["rwkv_k_a"][l]), "rk": row(p["rwkv_r_k"][l]), "lng": row(p["rwkv_lnx_g"][l]),
        "lnb": row(p["rwkv_lnx_b"][l]), "v0": row(v0),
        "w2": _pad_rows(p["rwkv_w2"][l], LANES, 0).astype(BF16),
        "a2": _pad_rows(p["rwkv_a2"][l], LANES, LORA_DECAY).astype(BF16),
        "g2": _pad_rows(p["rwkv_g2"][l], XG_W, 0).astype(BF16),
        "v2": v2.astype(BF16),
    }
    rw = jnp.pad(p["router_w"][l], ((0, 0), (0, LANES - N_EXPERTS)))
    rwh, rwl = _split_bf16(rw)
    rb = jnp.concatenate([p["router_b"][l], jnp.full((LANES - N_EXPERTS,), -1e30, F32)])[None, :]
    return {
        "w_proj": w_proj, "mu_ext": mu_ext, "pw": pw,
        "wa": p["w_up_a"][l].astype(BF16), "wb": p["w_up_b"][l].astype(BF16), "wo": p["w_out"][l].astype(BF16),
        "rwh": rwh, "rwl": rwl, "rb": rb,
        "n1g": row(p["norm1_g"][l]), "n2g": row(p["norm2_g"][l]), "gn_g": row(p["ret_gn_g"][l]),
    }


def _routing_tables(meta, counts, n_tok):
    blk = MOE_ROWS
    idx = meta[:, :TOP_K]
    rank = meta[:, TOP_K:2 * TOP_K]
    cnt = counts[0, :N_EXPERTS]
    pcnt = (cnt + blk - 1) // blk * blk
    pends = jnp.cumsum(pcnt)
    pstarts = pends - pcnt
    dest = (pstarts[idx] + rank).astype(I32)
    n_pad = n_tok * TOP_K + N_EXPERTS * blk
    nblk = n_pad // blk
    tok = jnp.broadcast_to(jnp.arange(n_tok, dtype=I32)[:, None], (n_tok, TOP_K))
    tok_tab = jnp.zeros((n_pad,), I32).at[dest.reshape(-1)].set(tok.reshape(-1), unique_indices=True)
    starts = jnp.arange(nblk, dtype=I32) * blk
    blk_e = jnp.minimum(jnp.sum((pends[None, :] <= starts[:, None]).astype(I32), axis=1), N_EXPERTS - 1)
    n_used = (pends[-1] // blk).astype(I32).reshape(1)
    ids = jnp.arange(N_EXPERTS, dtype=I32)
    used = cnt > 0
    later_used = (ids[None, :] > ids[:, None]) & used[None, :]
    nxt = jnp.min(jnp.where(later_used, ids[None, :], N_EXPERTS), axis=1)
    nxt = jnp.where(nxt >= N_EXPERTS, -1, nxt).astype(I32)
    slot = ((jnp.cumsum(used.astype(I32)) - 1) % 2).astype(I32)
    return dest.reshape(-1), tok_tab, blk_e, nxt[blk_e], slot[blk_e], n_used


def kernel(x, c, positions, ada_w, ada_b, norm1_g, norm2_g, w_in, shift_mu, rwkv_w0, rwkv_w2, rwkv_a0, rwkv_a2, rwkv_g2, rwkv_kk_scale, rwkv_k_a, rwkv_r_k, rwkv_lnx_g, rwkv_lnx_b, vres_mu, vres_w1, vres_w2, vres_v0, ret_gn_g, w_up_a, w_up_b, w_out, router_w, router_b, exp_w_gate, exp_b_gate, exp_w_up, exp_b_up, exp_w_down, exp_b_down, final_g):
    p = dict(w_in=w_in, shift_mu=shift_mu, rwkv_w0=rwkv_w0, rwkv_w2=rwkv_w2, rwkv_a0=rwkv_a0,
             rwkv_a2=rwkv_a2, rwkv_g2=rwkv_g2, rwkv_kk_scale=rwkv_kk_scale, rwkv_k_a=rwkv_k_a,
             rwkv_r_k=rwkv_r_k, rwkv_lnx_g=rwkv_lnx_g, rwkv_lnx_b=rwkv_lnx_b, vres_mu=vres_mu,
             vres_w1=vres_w1, vres_w2=vres_w2, vres_v0=vres_v0, ret_gn_g=ret_gn_g, w_up_a=w_up_a,
             w_up_b=w_up_b, w_out=w_out, router_w=router_w, router_b=router_b, norm1_g=norm1_g,
             norm2_g=norm2_g)
    batch, seq, d = x.shape
    n_layers = ada_w.shape[0]
    n_tok = batch * seq
    assert batch <= SUBLANES and d == D_MODEL

    c_pad = jnp.zeros((SUBLANES, d), F32).at[:batch].set(c)
    mod = _modulation(c_pad, ada_w, ada_b)[:, :batch].reshape(n_layers, batch, N_ADA, 1, d)

    half = RET_HEAD_DIM // 2
    inv_freq = ROPE_BASE ** (-jnp.arange(half, dtype=F32) / half)
    invf2 = jnp.concatenate([inv_freq, inv_freq])[None, :]
    cos2, sin2 = _rotary_tables(positions.reshape(n_tok, 1), invf2)
    log_gamma = jnp.log1p(-jnp.exp2(-5.0 - jnp.arange(RET_HEADS, dtype=F32)))
    log_gamma = jnp.broadcast_to(log_gamma[:, None, None], (RET_HEADS, 1, LANES))

    xf = x.reshape(n_tok, d)
    v_first = None
    for l in range(n_layers):
        lp = _layer_params(l, p)
        sh1, sc1, gt1, sh2, sc2, gt2 = (mod[l, :, i] for i in range(N_ADA))
        proj = _in_proj(xf, lp["n1g"], sc1, sh1, lp["w_proj"], seq)
        ya, v_first = _rwkv_mix(proj, lp["mu_ext"], lp["pw"], v_first, batch, seq)
        yb = _retention(proj, cos2, sin2, log_gamma, lp["gn_g"], batch, seq)
        xf, h2, logits = _merge_out(ya, yb, proj, xf, gt1, lp["n2g"], sc2, sh2, lp["wa"], lp["wb"],
                                    lp["wo"], lp["rwh"], lp["rwl"], lp["rb"], seq)
        meta, gates, counts = _route(logits)
        dest, tok_tab, blk_e, nxt_e, w_slot, n_used = _routing_tables(meta, counts, n_tok)
        ys = _moe_experts(l, blk_e, nxt_e, w_slot, n_used, tok_tab, h2, exp_w_gate, exp_b_gate, exp_w_up,
                          exp_b_up, exp_w_down, exp_b_down)
        xf = _moe_combine(dest, ys, xf, gates, gt2, final_g.reshape(1, d), seq, l == n_layers - 1)
    return xf.reshape(batch, seq, d)
```

```python
import functools
import math

import jax
import jax.numpy as jnp
from jax import lax
from jax.experimental import pallas as pl
from jax.experimental.pallas import tpu as pltpu

F32 = jnp.float32
BF16 = jnp.bfloat16
I32 = jnp.int32

D_MODEL = 2048
RWKV_HEAD_DIM = 64
RWKV_WIDTH = 1024
LORA_DECAY = 64
LORA_A = 64
LORA_VRES = 32
LORA_GATE = 160
RWKV_GN_EPS = 64e-5
RET_HEADS = 8
RET_HEAD_DIM = 128
RET_WIDTH = 1024
RET_GN_EPS = 1e-5
ROPE_BASE = 10000.0
N_EXPERTS = 32
TOP_K = 4
D_EXPERT = 512
SWIGLU_ALPHA = 1.702
SWIGLU_LIMIT = 7.0
NORM_EPS = 1e-6
N_ADA = 6
N_SHIFT = 3 * RWKV_WIDTH + LORA_DECAY + LORA_A + LORA_GATE
N_RET = 4 * RET_WIDTH

LANES = 128
SUBLANES = 8
MXU_DIM = 256

RET_OFF = 0
GATE_OFF = RET_OFF + N_RET
RW_OFF = GATE_OFF + 2 * D_MODEL
XG_OFF = RW_OFF + 3 * RWKV_WIDTH
XG_W = 256
XWA_OFF = XG_OFF + XG_W
XV_OFF = XWA_OFF + LANES
N_PROJ = XV_OFF + LANES

RWKV_CHUNK = 64
RWKV_ROWS = 256
RWKV_GROUPS_PER_BODY = 4
RET_ROWS = 256
MOE_ROWS = 256
MOE_WEIGHT_COPIES = 4
MOE_ISSUE_SPLIT = (96, 192, 224)
COMB_ROWS = 128
VMEM_LIMIT = 56 * 1024 * 1024


def _dot(a, b):
    return jnp.dot(a, b, preferred_element_type=F32)


def _dot_nt(a, b):
    return lax.dot_general(a, b, (((1,), (1,)), ((), ())), preferred_element_type=F32)


def _split_bf16(x):
    hi = x.astype(BF16)
    lo = (x - hi.astype(F32)).astype(BF16)
    return hi, lo


def _sigmoid(x):
    return 1.0 / (1.0 + jnp.exp(-x))


def _cparams(sem):
    return pltpu.CompilerParams(dimension_semantics=sem, vmem_limit_bytes=VMEM_LIMIT)


def _mod_kernel(c_ref, w_ref, b_ref, o_ref):
    c = c_ref[...]
    cs = c * _sigmoid(c)
    o_ref[0] = _dot(cs.astype(BF16), w_ref[0].astype(BF16)) + b_ref[0]


def _modulation(c_pad, ada_w, ada_b):
    n_layers, d, n = ada_w.shape
    tn = 1024
    return pl.pallas_call(
        _mod_kernel,
        grid=(n_layers, n // tn),
        in_specs=[
            pl.BlockSpec((SUBLANES, d), lambda l, j: (0, 0)),
            pl.BlockSpec((1, d, tn), lambda l, j: (l, 0, j)),
            pl.BlockSpec((1, 1, tn), lambda l, j: (l, 0, j)),
        ],
        out_specs=pl.BlockSpec((1, SUBLANES, tn), lambda l, j: (l, 0, j)),
        out_shape=jax.ShapeDtypeStruct((n_layers, SUBLANES, n), F32),
        compiler_params=_cparams(("parallel", "parallel")),
        name="adaln_mod",
    )(c_pad, ada_w, ada_b.reshape(n_layers, 1, n))


def _trig_kernel(pos_ref, invf_ref, cos_ref, sin_ref):
    ang = pos_ref[...].astype(F32) * invf_ref[...]
    lane = lax.broadcasted_iota(I32, ang.shape, 1)
    s = jnp.sin(ang)
    cos_ref[...] = jnp.cos(ang)
    sin_ref[...] = jnp.where(lane < RET_HEAD_DIM // 2, -s, s)


def _rotary_tables(pos_col, invf2):
    t = pos_col.shape[0]
    tm = 1024
    return pl.pallas_call(
        _trig_kernel,
        grid=(t // tm,),
        in_specs=[pl.BlockSpec((tm, 1), lambda i: (i, 0)), pl.BlockSpec((1, LANES), lambda i: (0, 0))],
        out_specs=[pl.BlockSpec((tm, LANES), lambda i: (i, 0))] * 2,
        out_shape=[jax.ShapeDtypeStruct((t, LANES), F32)] * 2,
        compiler_params=_cparams(("parallel",)),
        name="rotary_tables",
    )(pos_col, invf2)


def _in_kernel(x_ref, g_ref, sc_ref, sh_ref, w_ref, o_ref, h_scr):
    @pl.when(pl.program_id(1) == 0)
    def _():
        x = x_ref[...]
        y = x * lax.rsqrt(jnp.mean(x * x, axis=-1, keepdims=True) + NORM_EPS)
        h_scr[...] = (y * g_ref[...] * (1.0 + sc_ref[0]) + sh_ref[0]).astype(BF16)

    o_ref[...] = _dot(h_scr[...], w_ref[...])


def _in_proj(x, g, sc, sh, w, seq):
    t, d = x.shape
    n = w.shape[1]
    tm = min(1024, seq)
    tn = 512
    return pl.pallas_call(
        _in_kernel,
        grid=(t // tm, n // tn),
        in_specs=[
            pl.BlockSpec((tm, d), lambda i, j: (i, 0)),
            pl.BlockSpec((1, d), lambda i, j: (0, 0)),
            pl.BlockSpec((1, 1, d), lambda i, j: (i * tm // seq, 0, 0)),
            pl.BlockSpec((1, 1, d), lambda i, j: (i * tm // seq, 0, 0)),
            pl.BlockSpec((d, tn), lambda i, j: (0, j)),
        ],
        out_specs=pl.BlockSpec((tm, tn), lambda i, j: (i, j)),
        out_shape=jax.ShapeDtypeStruct((t, n), F32),
        scratch_shapes=[pltpu.VMEM((tm, d), BF16)],
        compiler_params=_cparams(("parallel", "arbitrary")),
        name="norm_in_proj",
    )(x, g, sc, sh, w)


def _bdot(a, b):
    return jnp.einsum("bij,bjk->bik", a, b, preferred_element_type=F32)


def _bdot_nt(a, b):
    return jnp.einsum("bik,bjk->bij", a, b, preferred_element_type=F32)


def _rwkv_chunk_terms(at, rt, bt, kt, bh, kh, v, wc_row):
    c = RWKV_CHUNK
    n = RWKV_HEAD_DIM
    lw = LANES
    nb = at.shape[0]
    reps = lw // c
    row = lax.broadcasted_iota(I32, (nb, lw, lw), 1)
    col = lax.broadcasted_iota(I32, (nb, lw, lw), 2)
    same_head = (row // n) == (col // n)

    def stack(x):
        return jnp.where(same_head, jnp.concatenate([x] * reps, axis=1), 0.0)

    def transpose(x):
        return jnp.stack([x[i].T for i in range(nb)])

    at2 = stack(at)
    rt2 = stack(rt)
    v2 = stack(v).astype(BF16)
    lhs = jnp.concatenate([at2, rt2], axis=1).astype(BF16)
    rhs = jnp.concatenate([stack(bt), stack(kt)], axis=1).astype(BF16)
    aa = _bdot_nt(lhs, rhs)
    strict = row > col
    lower = row >= col
    a_ab = jnp.where(strict, aa[:, :lw, :lw], 0.0)
    a_ak = jnp.where(strict, aa[:, :lw, lw:], 0.0)
    a_rb = jnp.where(lower, aa[:, lw:, :lw], 0.0)
    a_rk = jnp.where(lower, aa[:, lw:, lw:], 0.0)

    x = jnp.where(row == col, 1.0, 0.0) + jnp.where(((row ^ col) == 1) & ((row & 1) == 1), a_ab, 0.0)
    b = 2
    while b < c:
        sel = ((row // (2 * b)) == (col // (2 * b))) & ((row & b) != 0) & ((col & b) == 0)
        xb = x.astype(BF16)
        y = _bdot(xb, jnp.where(sel, a_ab, 0.0).astype(BF16))
        x = x + _bdot(y.astype(BF16), xb)
        b *= 2

    akv = _bdot(a_ak.astype(BF16), v2)
    p12b = _bdot(x.astype(BF16), jnp.concatenate([at2, akv], axis=2).astype(BF16)).astype(BF16)
    arb = _bdot(a_rb.astype(BF16), p12b)
    qp = (rt2 + arb[:, :, :lw]).astype(BF16)
    o_in = arb[:, :, lw:] + _bdot(a_rk.astype(BF16), v2)
    bp = _bdot(transpose(stack(bh)).astype(BF16), p12b)
    ht = bp[:, :, lw:] + _bdot(transpose(stack(kh)).astype(BF16), v2)
    wc_col = jnp.sum(jnp.where(row == col, wc_row, 0.0), axis=2, keepdims=True)
    return qp, o_in, bp[:, :, :lw].astype(BF16), ht, wc_col


def _rwkv_kernel(*refs, rows, has_vres):
    c = RWKV_CHUNK
    n = RWKV_HEAD_DIM
    lw = LANES
    ng = RWKV_WIDTH // lw
    nc = rows // c
    gpb = RWKV_GROUPS_PER_BODY
    it = iter(refs)
    r_ref, k_ref, v_ref, xg_ref, xwa_ref, xv_ref = (next(it) for _ in range(6))
    mu_r, mu_k, mu_v, mu_xg, mu_xwa, mu_xv = (next(it) for _ in range(6))
    w0, a0, kks, ka, rk, lng, lnb, v0 = (next(it) for _ in range(8))
    w2_ref, a2_ref, g2_ref, v2_ref = (next(it) for _ in range(4))
    vf_ref = next(it) if has_vres else None
    ya_ref = next(it)
    vfo_ref = None if has_vres else next(it)
    c_r, c_k, c_v, c_xg, c_xwa, c_xv = (next(it) for _ in range(6))
    st_s = next(it)
    at_s, rt_s, bt_s, kt_s, bh_s, kh_s, vv_s, wc_s, o_s = (next(it) for _ in range(9))
    bonus_s, gate_s = next(it), next(it)

    @pl.when(pl.program_id(1) == 0)
    def _():
        for cr in (c_r, c_k, c_v, c_xg, c_xwa, c_xv):
            cr[...] = jnp.zeros_like(cr)
        st_s[...] = jnp.zeros_like(st_s)

    def shift(p_ref, mu_ref, carry_ref):
        p = p_ref[...]
        row = lax.broadcasted_iota(I32, p.shape, 0)
        prev = jnp.where(row == 0, carry_ref[0:1, :], pltpu.roll(p, 1, 0))
        carry_ref[0:1, :] = p[rows - 1:rows, :]
        return p + (prev - p) * mu_ref[...]

    r = shift(r_ref, mu_r, c_r)
    k = shift(k_ref, mu_k, c_k)
    v = shift(v_ref, mu_v, c_v)
    xg = shift(xg_ref, mu_xg, c_xg)
    xwa = shift(xwa_ref, mu_xwa, c_xwa)

    z = w0[...] + _dot(jnp.tanh(xwa).astype(BF16), w2_ref[...])
    lw_log = -math.exp(-0.5) * _sigmoid(z)
    a = _sigmoid(a0[...] + _dot(xwa.astype(BF16), a2_ref[...]))
    gate_s[...] = _dot(_sigmoid(xg).astype(BF16), g2_ref[...])
    if has_vres:
        xv = shift(xv_ref, mu_xv, c_xv)
        v = v + (vf_ref[...] - v) * _sigmoid(v0[...] + _dot(xv.astype(BF16), v2_ref[...]))
    else:
        vfo_ref[...] = v

    hrow = lax.broadcasted_iota(I32, (MXU_DIM, MXU_DIM), 0) // n
    hcol = lax.broadcasted_iota(I32, (MXU_DIM, MXU_DIM), 1) // n
    head_ones = (hrow == hcol).astype(BF16)

    def headsum(x):
        xb = x.astype(BF16)
        return jnp.concatenate([_dot(xb[:, i * MXU_DIM:(i + 1) * MXU_DIM], head_ones)
                                for i in range(RWKV_WIDTH // MXU_DIM)], axis=1)

    kk0 = k * kks[...]
    kk = kk0 * lax.rsqrt(jnp.maximum(headsum(kk0 * kk0), 1e-24))
    km = k * (1.0 + (a - 1.0) * ka[...])
    bonus_s[...] = headsum(r * km * rk[...]) * v

    trow = lax.broadcasted_iota(I32, (rows, rows), 0) // c
    tcol = lax.broadcasted_iota(I32, (rows, rows), 1) // c
    same_chunk = trow == tcol
    tri = (same_chunk & (lax.broadcasted_iota(I32, (rows, rows), 1)
                         <= lax.broadcasted_iota(I32, (rows, rows), 0))).astype(BF16)
    blk_ones = same_chunk.astype(BF16)
    lhi, llo = _split_bf16(lw_log)
    cs = _dot(tri, lhi) + _dot(tri, llo)
    ctot = _dot(blk_ones, lhi) + _dot(blk_ones, llo)
    b_vec = kk * a
    w_inv = jnp.exp(-cs)
    tail = jnp.exp(ctot - cs)
    planes = ((at_s, -kk * jnp.exp(cs - lw_log)), (rt_s, r * jnp.exp(cs)), (bt_s, b_vec * w_inv),
              (kt_s, km * w_inv), (bh_s, b_vec * tail), (kh_s, km * tail), (vv_s, v), (wc_s, jnp.exp(ctot)))
    for dst, val in planes:
        for g in range(ng):
            dst[g] = val[:, g * lw:(g + 1) * lw]

    def pair_body(gp, carry):
        gs = pl.ds(gp * gpb, gpb)

        def chunks(ref):
            return ref[gs].reshape(gpb * nc, c, lw)

        qp_b, o_in_b, bpg_b, ht_b, wc_col_b = _rwkv_chunk_terms(
            chunks(at_s), chunks(rt_s), chunks(bt_s), chunks(kt_s), chunks(bh_s), chunks(kh_s),
            chunks(vv_s), chunks(wc_s)[:, 0:1, :])
        states = [st_s[gp * gpb + gg] for gg in range(gpb)]
        for ci in range(nc):
            for gg in range(gpb):
                bi = gg * nc + ci
                qp, o_in, bpg, ht, wc_col = qp_b[bi], o_in_b[bi], bpg_b[bi], ht_b[bi], wc_col_b[bi]
                st = states[gg]
                stb = st.astype(BF16)
                o2 = _dot(qp, stb) + o_in
                o = o2[0:c, :]
                for h in range(1, lw // c):
                    o = o + o2[h * c:(h + 1) * c, :]
                o_s[gp * gpb + gg, pl.ds(ci * c, c), :] = o
                states[gg] = wc_col * st + _dot(bpg, stb) + ht
        for gg in range(gpb):
            st_s[gp * gpb + gg] = states[gg]
        return carry

    lax.fori_loop(0, ng // gpb, pair_body, 0)

    o = jnp.concatenate([o_s[g] for g in range(ng)], axis=1)
    mean = headsum(o) * (1.0 / n)
    dev = o - mean
    var = headsum(dev * dev) * (1.0 / n)
    on = dev * lax.rsqrt(var + RWKV_GN_EPS) * lng[...] + lnb[...]
    ya_ref[...] = ((on + bonus_s[...]) * gate_s[...]).astype(ya_ref.dtype)


def _rwkv_mix(proj, mu_ext, pw, v_first, batch, seq):
    t = proj.shape[0]
    rows = min(RWKV_ROWS, seq)
    w = RWKV_WIDTH
    nj = seq // rows
    ng = w // LANES
    has_vres = v_first is not None

    def colspec(width, off):
        return pl.BlockSpec((rows, width), lambda b, j: (b * nj + j, off // width))

    def muspec(width, off):
        return pl.BlockSpec((1, width), lambda b, j: (0, off // width))

    layout = [(w, RW_OFF), (w, RW_OFF + w), (w, RW_OFF + 2 * w), (XG_W, XG_OFF), (LANES, XWA_OFF), (LANES, XV_OFF)]
    in_specs = [colspec(*a) for a in layout] + [muspec(*a) for a in layout]
    args = [proj] * 6 + [mu_ext] * 6
    for name in ("w0", "a0", "kks", "ka", "rk", "lng", "lnb", "v0"):
        in_specs.append(pl.BlockSpec((1, w), lambda b, j: (0, 0)))
        args.append(pw[name])
    for name, kdim in (("w2", LANES), ("a2", LANES), ("g2", XG_W), ("v2", LANES)):
        in_specs.append(pl.BlockSpec((kdim, w), lambda b, j: (0, 0)))
        args.append(pw[name])
    io_spec = pl.BlockSpec((rows, w), lambda b, j: (b * nj + j, 0))
    out_specs = [io_spec]
    out_shape = [jax.ShapeDtypeStruct((t, w), BF16)]
    if has_vres:
        in_specs.append(io_spec)
        args.append(v_first)
    else:
        out_specs.append(io_spec)
        out_shape.append(jax.ShapeDtypeStruct((t, w), F32))
    scratch = [pltpu.VMEM((SUBLANES, w), F32)] * 3 + [pltpu.VMEM((SUBLANES, XG_W), F32)]
    scratch += [pltpu.VMEM((SUBLANES, LANES), F32)] * 2
    scratch += [pltpu.VMEM((ng, LANES, LANES), F32)]
    scratch += [pltpu.VMEM((ng, rows, LANES), F32)] * 9
    scratch += [pltpu.VMEM((rows, w), F32)] * 2
    outs = pl.pallas_call(
        functools.partial(_rwkv_kernel, rows=rows, has_vres=has_vres),
        grid=(batch, nj),
        in_specs=in_specs,
        out_specs=out_specs,
        out_shape=out_shape,
        scratch_shapes=scratch,
        compiler_params=_cparams(("parallel", "arbitrary")),
        name="rwkv7_mix",
    )(*args)
    if has_vres:
        return outs[0], v_first
    return outs[0], outs[1]


def _ret_kernel(q_ref, k_ref, v_ref, g_ref, cos_ref, sin_ref, lg_ref, gn_ref, o_ref, st_ref, *, rows):
    @pl.when(pl.program_id(1) == 0)
    def _():
        st_ref[...] = jnp.zeros_like(st_ref)

    hd = RET_HEAD_DIM
    cos = cos_ref[...]
    sin = sin_ref[...]
    row = lax.broadcasted_iota(I32, (rows, rows), 0)
    col = lax.broadcasted_iota(I32, (rows, rows), 1)
    causal = row >= col
    rel = jnp.maximum(row - col, 0).astype(F32)
    idx = lax.broadcasted_iota(I32, (rows, LANES), 0).astype(F32)

    def rot(x):
        return x * cos + pltpu.roll(x, hd // 2, 1) * sin

    for h in range(RET_HEADS):
        hs = slice(h * hd, (h + 1) * hd)
        lg = lg_ref[h]
        q = rot(q_ref[:, hs])
        k = rot(k_ref[:, hs]) * (hd ** -0.5)
        v = v_ref[:, hs].astype(BF16)
        lg_sq = jnp.concatenate([lg] * (rows // LANES), axis=1) if rows > LANES else lg
        decay = jnp.where(causal, jnp.exp(rel * lg_sq), 0.0)
        qb = q.astype(BF16)
        scores = _dot_nt(qb, k.astype(BF16)) * decay
        intra = _dot(scores.astype(BF16), v)
        xi = jnp.exp((idx + 1.0) * lg)
        st = st_ref[h]
        cross = _dot(qb, st.astype(BF16)) * xi
        zeta = jnp.exp((rows - 1.0 - idx) * lg)
        kz = (k * zeta).T.astype(BF16)
        chunk_decay = jnp.exp(rows * lg)
        st_ref[h] = st * chunk_decay + _dot(kz, v)
        o = intra + cross
        mean = jnp.mean(o, axis=-1, keepdims=True)
        dev = o - mean
        var = jnp.mean(dev * dev, axis=-1, keepdims=True)
        on = dev * lax.rsqrt(var + RET_GN_EPS) * gn_ref[:, hs]
        gr = g_ref[:, hs]
        o_ref[:, hs] = (gr * _sigmoid(gr) * on).astype(o_ref.dtype)


def _retention(proj, cos2, sin2, log_gamma, gn_g, batch, seq):
    t = proj.shape[0]
    rows = min(RET_ROWS, seq)
    nj = seq // rows
    w = RET_WIDTH

    def pspec(off):
        return pl.BlockSpec((rows, w), lambda b, j: (b * nj + j, off // w))

    tspec = pl.BlockSpec((rows, LANES), lambda b, j: (b * nj + j, 0))
    return pl.pallas_call(
        functools.partial(_ret_kernel, rows=rows),
        grid=(batch, nj),
        in_specs=[pspec(RET_OFF), pspec(RET_OFF + w), pspec(RET_OFF + 2 * w), pspec(RET_OFF + 3 * w),
                  tspec, tspec,
                  pl.BlockSpec((RET_HEADS, 1, LANES), lambda b, j: (0, 0, 0)),
                  pl.BlockSpec((1, w), lambda b, j: (0, 0))],
        out_specs=pl.BlockSpec((rows, w), lambda b, j: (b * nj + j, 0)),
        out_shape=jax.ShapeDtypeStruct((t, w), BF16),
        scratch_shapes=[pltpu.VMEM((RET_HEADS, RET_HEAD_DIM, RET_HEAD_DIM), F32)],
        compiler_params=_cparams(("parallel", "arbitrary")),
        name="retention",
    )(proj, proj, proj, proj, cos2, sin2, log_gamma, gn_g)


def _out_kernel(ya_ref, yb_ref, ga_ref, gb_ref, x_ref, gt1_ref, n2g_ref, sc2_ref, sh2_ref,
                wa_ref, wb_ref, wo_ref, rwh_ref, rwl_ref, rb_ref, xo_ref, h2_ref, lg_ref):
    ua = _dot(ya_ref[...], wa_ref[...])
    ub = _dot(yb_ref[...], wb_ref[...])
    y = _sigmoid(ga_ref[...]) * ua + _sigmoid(gb_ref[...]) * ub
    xn = x_ref[...] + gt1_ref[0] * _dot(y.astype(BF16), wo_ref[...])
    xo_ref[...] = xn
    yn = xn * lax.rsqrt(jnp.mean(xn * xn, axis=-1, keepdims=True) + NORM_EPS)
    h2 = yn * n2g_ref[...] * (1.0 + sc2_ref[0]) + sh2_ref[0]
    h2_ref[...] = h2
    hi, lo = _split_bf16(h2)
    lg_ref[...] = _dot(hi, rwh_ref[...]) + _dot(hi, rwl_ref[...]) + _dot(lo, rwh_ref[...]) + rb_ref[...]


def _merge_out(ya, yb, proj, x, gt1, n2g, sc2, sh2, wa, wb, wo, rwh, rwl, rb, seq):
    t, d = x.shape
    tm = 256
    nb = seq // tm
    const = lambda i: (0, 0)
    mod = pl.BlockSpec((1, 1, d), lambda i: (i // nb, 0, 0))
    single = pl.Buffered(1)
    return pl.pallas_call(
        _out_kernel,
        grid=(t // tm,),
        in_specs=[
            pl.BlockSpec((tm, RWKV_WIDTH), lambda i: (i, 0)),
            pl.BlockSpec((tm, RET_WIDTH), lambda i: (i, 0)),
            pl.BlockSpec((tm, d), lambda i: (i, GATE_OFF // d)),
            pl.BlockSpec((tm, d), lambda i: (i, GATE_OFF // d + 1)),
            pl.BlockSpec((tm, d), lambda i: (i, 0)),
            mod, pl.BlockSpec((1, d), const), mod, mod,
            pl.BlockSpec((RWKV_WIDTH, d), const, pipeline_mode=single),
            pl.BlockSpec((RET_WIDTH, d), const, pipeline_mode=single),
            pl.BlockSpec((d, d), const, pipeline_mode=single),
            pl.BlockSpec((d, LANES), const, pipeline_mode=single),
            pl.BlockSpec((d, LANES), const, pipeline_mode=single),
            pl.BlockSpec((1, LANES), const),
        ],
        out_specs=[pl.BlockSpec((tm, d), lambda i: (i, 0)), pl.BlockSpec((tm, d), lambda i: (i, 0)),
                   pl.BlockSpec((tm, LANES), lambda i: (i, 0))],
        out_shape=[jax.ShapeDtypeStruct((t, d), F32), jax.ShapeDtypeStruct((t, d), F32),
                   jax.ShapeDtypeStruct((t, LANES), F32)],
        compiler_params=_cparams(("parallel",)),
        name="merge_out_norm_router",
    )(ya, yb, proj, proj, x, gt1, n2g, sc2, sh2, wa, wb, wo, rwh, rwl, rb)


def _route_kernel(lg_ref, meta_ref, gate_ref, cnt_ref, carry_ref, *, tm):
    @pl.when(pl.program_id(0) == 0)
    def _():
        carry_ref[...] = jnp.zeros_like(carry_ref)

    l = lg_ref[...]
    lane = lax.broadcasted_iota(I32, l.shape, 1)
    lane_f = lane.astype(F32)
    vals, idxs = [], []
    for _ in range(TOP_K):
        m = jnp.max(l, axis=-1, keepdims=True)
        idx = jnp.min(jnp.where(l == m, lane_f, float(LANES)), axis=-1, keepdims=True)
        vals.append(m)
        idxs.append(idx)
        l = jnp.where(lane_f == idx, -jnp.inf, l)
    es = [jnp.exp(vv - vals[0]) for vv in vals]
    denom = es[0] + es[1] + es[2] + es[3]
    sel = jnp.zeros(l.shape, F32)
    for idx in idxs:
        sel = sel + jnp.where(lane_f == idx, 1.0, 0.0)
    row = lax.broadcasted_iota(I32, (tm, tm), 0)
    col = lax.broadcasted_iota(I32, (tm, tm), 1)
    tri = (col <= row).astype(BF16)
    cum = _dot(tri, sel.astype(BF16)) + carry_ref[0:1, :]
    excl = cum - sel
    carry_ref[0:1, :] = cum[tm - 1:tm, :]
    meta = jnp.zeros(l.shape, F32)
    gates = jnp.zeros(l.shape, F32)
    for kk in range(TOP_K):
        rank = jnp.sum(jnp.where(lane_f == idxs[kk], excl, 0.0), axis=-1, keepdims=True)
        meta = meta + jnp.where(lane == kk, idxs[kk], 0.0) + jnp.where(lane == TOP_K + kk, rank, 0.0)
        gates = gates + jnp.where(lane == kk, es[kk] / denom, 0.0)
    meta_ref[...] = meta.astype(I32)
    gate_ref[...] = gates
    cnt_ref[...] = jnp.broadcast_to(cum[tm - 1:tm, :], cnt_ref.shape).astype(I32)


def _route(logits):
    t = logits.shape[0]
    tm = 256
    return pl.pallas_call(
        functools.partial(_route_kernel, tm=tm),
        grid=(t // tm,),
        in_specs=[pl.BlockSpec((tm, LANES), lambda i: (i, 0))],
        out_specs=[pl.BlockSpec((tm, LANES), lambda i: (i, 0)), pl.BlockSpec((tm, LANES), lambda i: (i, 0)),
                   pl.BlockSpec((SUBLANES, LANES), lambda i: (0, 0))],
        out_shape=[jax.ShapeDtypeStruct((t, LANES), I32), jax.ShapeDtypeStruct((t, LANES), F32),
                   jax.ShapeDtypeStruct((SUBLANES, LANES), I32)],
        scratch_shapes=[pltpu.VMEM((SUBLANES, LANES), F32)],
        compiler_params=_cparams(("arbitrary",)),
        name="router_topk",
    )(logits)


def _moe_kernel(be_ref, nx_ref, ws_ref, nu_ref, tok_ref, h_hbm, wg_hbm, wu_hbm, wd_hbm, bg_ref, bu_ref, bd_ref,
                ys_ref, xbuf, xb_s, gt_s, act_s, wg_buf, wu_buf, wd_buf, wgu_s, wd_s, sem, wsem, *, layer):
    i = pl.program_id(0)
    n_used = nu_ref[0]
    blk = MOE_ROWS
    f = D_EXPERT
    e = be_ref[i]
    wslot = ws_ref[i]

    def row_copy(tok, r, slot):
        return pltpu.make_async_copy(h_hbm.at[pl.ds(tok, 1), :], xbuf.at[slot, pl.ds(r, 1), :], sem.at[slot])

    def wait_block(slot):
        pltpu.make_async_copy(h_hbm.at[pl.ds(0, blk), :], xbuf.at[slot], sem.at[slot]).wait()

    def weight_copies(expert, slot):
        copies = []
        for src, dst in ((wg_hbm, wg_buf), (wu_hbm, wu_buf), (wd_hbm, wd_buf)):
            rows = src.shape[2] // MOE_WEIGHT_COPIES
            for part in range(MOE_WEIGHT_COPIES):
                rs = pl.ds(part * rows, rows)
                copies.append(pltpu.make_async_copy(src.at[layer, expert, rs, :], dst.at[slot, rs, :],
                                                    wsem.at[slot]))
        return copies

    @pl.when(i == 0)
    def _():
        for cp in weight_copies(e, wslot):
            cp.start()

        def body(r, carry):
            row_copy(tok_ref[r], r, 0).start()
            return carry
        lax.fori_loop(0, blk, body, 0)

    prev_e = be_ref[jnp.maximum(i - 1, 0)]

    @pl.when((i < n_used) & ((i == 0) | (e != prev_e)))
    def _():
        for cp in weight_copies(e, wslot):
            cp.wait()
        wgu_s[:, :f] = wg_buf[wslot].astype(BF16)
        wgu_s[:, f:] = wu_buf[wslot].astype(BF16)
        wd_s[...] = wd_buf[wslot].astype(BF16)
        nxt = nx_ref[i]

        @pl.when(nxt >= 0)
        def _():
            for cp in weight_copies(nxt, 1 - wslot):
                cp.start()

    slot = i % 2
    base = (i + 1) * blk
    half = D_MODEL // 2

    def issue_rows(lo, hi):
        for r in range(lo, hi):
            row_copy(tok_ref[base + r], r, 1 - slot).start()

    @pl.when(i < n_used)
    def _():
        wait_block(slot)
        xb_s[...] = xbuf[slot].astype(BF16)
        issue_rows(0, MOE_ISSUE_SPLIT[0])
        gt_s[...] = jnp.minimum(_dot(xb_s[...], wgu_s[:, :f]) + bg_ref[0, 0], SWIGLU_LIMIT)

    @pl.when(i + 1 <= n_used)
    def _():
        issue_rows(MOE_ISSUE_SPLIT[0], MOE_ISSUE_SPLIT[1])
        up = jnp.clip(_dot(xb_s[...], wgu_s[:, f:]) + bu_ref[0, 0], -SWIGLU_LIMIT, SWIGLU_LIMIT)
        gt = gt_s[...]
        act_s[...] = ((up + 1.0) * (gt * _sigmoid(gt * SWIGLU_ALPHA))).astype(BF16)

    @pl.when(n_used > i)
    def _():
        issue_rows(MOE_ISSUE_SPLIT[1], MOE_ISSUE_SPLIT[2])
        ys_ref[:, :half] = _dot(act_s[...], wd_s[:, :half]) + bd_ref[0, 0][:, :half]

    @pl.when(n_used - i >= 1)
    def _():
        issue_rows(MOE_ISSUE_SPLIT[2], blk)
        ys_ref[:, half:] = _dot(act_s[...], wd_s[:, half:]) + bd_ref[0, 0][:, half:]

    @pl.when(i == n_used)
    def _():
        wait_block(i % 2)

    @pl.when(i >= n_used)
    def _():
        ys_ref[...] = jnp.zeros_like(ys_ref)


def _moe_experts(layer, blk_e, nxt_e, w_slot, n_used, tok_tab, h2, w_gate, b_gate, w_up, b_up, w_down, b_down):
    d = D_MODEL
    f = D_EXPERT
    n_pad = tok_tab.shape[0]
    blk = MOE_ROWS
    nblk = n_pad // blk
    bmap = lambda i, be, nx, ws, nu, tk: (layer, be[i], 0, 0)
    grid_spec = pltpu.PrefetchScalarGridSpec(
        num_scalar_prefetch=5,
        grid=(nblk,),
        in_specs=[
            pl.BlockSpec(memory_space=pl.ANY),
            pl.BlockSpec(memory_space=pl.ANY),
            pl.BlockSpec(memory_space=pl.ANY),
            pl.BlockSpec(memory_space=pl.ANY),
            pl.BlockSpec((1, 1, 1, f), bmap),
            pl.BlockSpec((1, 1, 1, f), bmap),
            pl.BlockSpec((1, 1, 1, d), bmap),
        ],
        out_specs=pl.BlockSpec((blk, d), lambda i, be, nx, ws, nu, tk: (i, 0)),
        scratch_shapes=[pltpu.VMEM((2, blk, d), F32), pltpu.VMEM((blk, d), BF16),
                        pltpu.VMEM((blk, f), F32), pltpu.VMEM((blk, f), BF16),
                        pltpu.VMEM((2, d, f), F32), pltpu.VMEM((2, d, f), F32), pltpu.VMEM((2, f, d), F32),
                        pltpu.VMEM((d, 2 * f), BF16), pltpu.VMEM((f, d), BF16),
                        pltpu.SemaphoreType.DMA((2,)), pltpu.SemaphoreType.DMA((2,))],
    )
    n_layers, n_exp = w_gate.shape[:2]
    return pl.pallas_call(
        functools.partial(_moe_kernel, layer=layer),
        grid_spec=grid_spec,
        out_shape=jax.ShapeDtypeStruct((n_pad, d), F32),
        compiler_params=_cparams(("arbitrary",)),
        name="moe_experts",
    )(blk_e, nxt_e, w_slot, n_used, tok_tab, h2, w_gate, w_up, w_down, b_gate.reshape(n_layers, n_exp, 1, f),
      b_up.reshape(n_layers, n_exp, 1, f), b_down.reshape(n_layers, n_exp, 1, d))


def _comb_kernel(dest_ref, ys_hbm, x_ref, gate_ref, gt2_ref, fg_ref, o_ref, buf, sem, *, final, n_tiles):
    i = pl.program_id(0)
    tc = COMB_ROWS

    def row_copy(base, r, kk, slot):
        return pltpu.make_async_copy(ys_hbm.at[pl.ds(dest_ref[base + r * TOP_K + kk], 1), :],
                                     buf.at[slot, kk, pl.ds(r, 1), :], sem.at[slot])

    def wait_tile(slot):
        for kk in range(TOP_K):
            pltpu.make_async_copy(ys_hbm.at[pl.ds(0, tc), :], buf.at[slot, kk], sem.at[slot]).wait()

    @pl.when(i == 0)
    def _():
        def body(r, carry):
            for kk in range(TOP_K):
                row_copy(0, r, kk, 0).start()
            return carry
        lax.fori_loop(0, tc, body, 0)

    slot = i % 2
    wait_tile(slot)
    base = jnp.minimum(i + 1, n_tiles - 1) * (tc * TOP_K)
    for r in range(tc):
        for kk in range(TOP_K):
            row_copy(base, r, kk, 1 - slot).start()
    gates = gate_ref[...]
    moe = gates[:, 0:1] * buf[slot, 0]
    for kk in range(1, TOP_K):
        moe = moe + gates[:, kk:kk + 1] * buf[slot, kk]
    xn = x_ref[...] + gt2_ref[0] * moe
    if final:
        xn = xn * lax.rsqrt(jnp.mean(xn * xn, axis=-1, keepdims=True) + NORM_EPS) * fg_ref[...]
    o_ref[...] = xn

    @pl.when(i == n_tiles - 1)
    def _():
        wait_tile(1 - slot)


def _moe_combine(dest_flat, ys, x, gates, gt2, final_g, seq, final):
    t, d = x.shape
    tc = COMB_ROWS
    nb = seq // tc
    n_tiles = t // tc
    grid_spec = pltpu.PrefetchScalarGridSpec(
        num_scalar_prefetch=1,
        grid=(n_tiles,),
        in_specs=[
            pl.BlockSpec(memory_space=pl.ANY),
            pl.BlockSpec((tc, d), lambda i, ds: (i, 0)),
            pl.BlockSpec((tc, LANES), lambda i, ds: (i, 0)),
            pl.BlockSpec((1, 1, d), lambda i, ds: (i // nb, 0, 0)),
            pl.BlockSpec((1, d), lambda i, ds: (0, 0)),
        ],
        out_specs=pl.BlockSpec((tc, d), lambda i, ds: (i, 0)),
        scratch_shapes=[pltpu.VMEM((2, TOP_K, tc, d), F32), pltpu.SemaphoreType.DMA((2,))],
    )
    return pl.pallas_call(
        functools.partial(_comb_kernel, final=final, n_tiles=n_tiles),
        grid_spec=grid_spec,
        out_shape=jax.ShapeDtypeStruct((t, d), F32),
        compiler_params=_cparams(("arbitrary",)),
        name="moe_combine",
    )(dest_flat, ys, x, gates, gt2, final_g)


def _pad_rows(w, n_rows, at=0):
    out = jnp.zeros((n_rows, w.shape[1]), w.dtype)
    return out.at[at:at + w.shape[0]].set(w)


def _layer_params(l, p):
    d = D_MODEL
    w_in = p["w_in"][l]
    sh, ret, gate = w_in[:, :N_SHIFT], w_in[:, N_SHIFT:N_SHIFT + N_RET], w_in[:, N_SHIFT + N_RET:]
    rkv = sh[:, :3 * RWKV_WIDTH]
    xw = sh[:, 3 * RWKV_WIDTH:3 * RWKV_WIDTH + LORA_DECAY]
    xa = sh[:, 3 * RWKV_WIDTH + LORA_DECAY:3 * RWKV_WIDTH + LORA_DECAY + LORA_A]
    xg = sh[:, 3 * RWKV_WIDTH + LORA_DECAY + LORA_A:]
    xg = jnp.pad(xg, ((0, 0), (0, XG_W - LORA_GATE)))
    mu = p["shift_mu"][l]
    mu_xg = jnp.pad(mu[3 * RWKV_WIDTH + LORA_DECAY + LORA_A:], (0, XG_W - LORA_GATE))
    mu_xv = jnp.concatenate([jnp.zeros((LORA_VRES,), F32), jnp.ones((LORA_VRES,), F32),
                             jnp.zeros((LANES - 2 * LORA_VRES,), F32)])
    if l == 0:
        xv = jnp.zeros((d, LANES), F32)
        v2 = jnp.zeros((LANES, RWKV_WIDTH), F32)
        v0 = jnp.zeros((RWKV_WIDTH,), F32)
    else:
        mu_v = p["vres_mu"][l - 1][:, None]
        v1 = p["vres_w1"][l - 1]
        xv = jnp.concatenate([v1 * (1.0 - mu_v), v1 * mu_v, jnp.zeros((d, LANES - 2 * LORA_VRES), F32)], axis=1)
        w2v = p["vres_w2"][l - 1]
        v2 = _pad_rows(jnp.concatenate([w2v, w2v], axis=0), LANES)
        v0 = p["vres_v0"][l - 1]
    w_proj = jnp.concatenate([ret, gate, rkv, xg, xw, xa, xv], axis=1).astype(BF16)
    mu_ext = jnp.concatenate([jnp.zeros((N_RET + 2 * d,), F32), mu[:3 * RWKV_WIDTH], mu_xg,
                              mu[3 * RWKV_WIDTH:3 * RWKV_WIDTH + LORA_DECAY + LORA_A], mu_xv])[None, :]
    row = lambda a: a.reshape(1, -1)
    pw = {
        "w0": row(p["rwkv_w0"][l]), "a0": row(p["rwkv_a0"][l]), "kks": row(p["rwkv_kk_scale"][l]),
        "ka": row(p["rwkv_k_a"][l]), "rk": row(p["rwkv_r_k"][l]), "lng": row(p["rwkv_lnx_g"][l]),
        "lnb": row(p["rwkv_lnx_b"][l]), "v0": row(v0),
        "w2": _pad_rows(p["rwkv_w2"][l], LANES, 0).astype(BF16),
        "a2": _pad_rows(p["rwkv_a2"][l], LANES, LORA_DECAY).astype(BF16),
        "g2": _pad_rows(p["rwkv_g2"][l], XG_W, 0).astype(BF16),
        "v2": v2.astype(BF16),
    }
    rw = jnp.pad(p["router_w"][l], ((0, 0), (0, LANES - N_EXPERTS)))
    rwh, rwl = _split_bf16(rw)
    rb = jnp.concatenate([p["router_b"][l], jnp.full((LANES - N_EXPERTS,), -1e30, F32)])[None, :]
    return {
        "w_proj": w_proj, "mu_ext": mu_ext, "pw": pw,
        "wa": p["w_up_a"][l].astype(BF16), "wb": p["w_up_b"][l].astype(BF16), "wo": p["w_out"][l].astype(BF16),
        "rwh": rwh, "rwl": rwl, "rb": rb,
        "n1g": row(p["norm1_g"][l]), "n2g": row(p["norm2_g"][l]), "gn_g": row(p["ret_gn_g"][l]),
    }


def _routing_tables(meta, counts, n_tok):
    blk = MOE_ROWS
    idx = meta[:, :TOP_K]
    rank = meta[:, TOP_K:2 * TOP_K]
    cnt = counts[0, :N_EXPERTS]
    pcnt = (cnt + blk - 1) // blk * blk
    pends = jnp.cumsum(pcnt)
    pstarts = pends - pcnt
    dest = (pstarts[idx] + rank).astype(I32)
    n_pad = n_tok * TOP_K + N_EXPERTS * blk
    nblk = n_pad // blk
    tok = jnp.broadcast_to(jnp.arange(n_tok, dtype=I32)[:, None], (n_tok, TOP_K))
    tok_tab = jnp.zeros((n_pad,), I32).at[dest.reshape(-1)].set(tok.reshape(-1), unique_indices=True)
    starts = jnp.arange(nblk, dtype=I32) * blk
    blk_e = jnp.minimum(jnp.sum((pends[None, :] <= starts[:, None]).astype(I32), axis=1), N_EXPERTS - 1)
    n_used = (pends[-1] // blk).astype(I32).reshape(1)
    ids = jnp.arange(N_EXPERTS, dtype=I32)
    used = cnt > 0
    later_used = (ids[None, :] > ids[:, None]) & used[None, :]
    nxt = jnp.min(jnp.where(later_used, ids[None, :], N_EXPERTS), axis=1)
    nxt = jnp.where(nxt >= N_EXPERTS, -1, nxt).astype(I32)
    slot = ((jnp.cumsum(used.astype(I32)) - 1) % 2).astype(I32)
    return dest.reshape(-1), tok_tab, blk_e, nxt[blk_e], slot[blk_e], n_used


def kernel(x, c, positions, ada_w, ada_b, norm1_g, norm2_g, w_in, shift_mu, rwkv_w0, rwkv_w2, rwkv_a0, rwkv_a2, rwkv_g2, rwkv_kk_scale, rwkv_k_a, rwkv_r_k, rwkv_lnx_g, rwkv_lnx_b, vres_mu, vres_w1, vres_w2, vres_v0, ret_gn_g, w_up_a, w_up_b, w_out, router_w, router_b, exp_w_gate, exp_b_gate, exp_w_up, exp_b_up, exp_w_down, exp_b_down, final_g):
    p = dict(w_in=w_in, shift_mu=shift_mu, rwkv_w0=rwkv_w0, rwkv_w2=rwkv_w2, rwkv_a0=rwkv_a0,
             rwkv_a2=rwkv_a2, rwkv_g2=rwkv_g2, rwkv_kk_scale=rwkv_kk_scale, rwkv_k_a=rwkv_k_a,
             rwkv_r_k=rwkv_r_k, rwkv_lnx_g=rwkv_lnx_g, rwkv_lnx_b=rwkv_lnx_b, vres_mu=vres_mu,
             vres_w1=vres_w1, vres_w2=vres_w2, vres_v0=vres_v0, ret_gn_g=ret_gn_g, w_up_a=w_up_a,
             w_up_b=w_up_b, w_out=w_out, router_w=router_w, router_b=router_b, norm1_g=norm1_g,
             norm2_g=norm2_g)
    batch, seq, d = x.shape
    n_layers = ada_w.shape[0]
    n_tok = batch * seq
    assert batch <= SUBLANES and d == D_MODEL

    c_pad = jnp.zeros((SUBLANES, d), F32).at[:batch].set(c)
    mod = _modulation(c_pad, ada_w, ada_b)[:, :batch].reshape(n_layers, batch, N_ADA, 1, d)

    half = RET_HEAD_DIM // 2
    inv_freq = ROPE_BASE ** (-jnp.arange(half, dtype=F32) / half)
    invf2 = jnp.concatenate([inv_freq, inv_freq])[None, :]
    cos2, sin2 = _rotary_tables(positions.reshape(n_tok, 1), invf2)
    log_gamma = jnp.log1p(-jnp.exp2(-5.0 - jnp.arange(RET_HEADS, dtype=F32)))
    log_gamma = jnp.broadcast_to(log_gamma[:, None, None], (RET_HEADS, 1, LANES))

    xf = x.reshape(n_tok, d)
    v_first = None
    for l in range(n_layers):
        lp = _layer_params(l, p)
        sh1, sc1, gt1, sh2, sc2, gt2 = (mod[l, :, i] for i in range(N_ADA))
        proj = _in_proj(xf, lp["n1g"], sc1, sh1, lp["w_proj"], seq)
        ya, v_first = _rwkv_mix(proj, lp["mu_ext"], lp["pw"], v_first, batch, seq)
        yb = _retention(proj, cos2, sin2, log_gamma, lp["gn_g"], batch, seq)
        xf, h2, logits = _merge_out(ya, yb, proj, xf, gt1, lp["n2g"], sc2, sh2, lp["wa"], lp["wb"],
                                    lp["wo"], lp["rwh"], lp["rwl"], lp["rb"], seq)
        meta, gates, counts = _route(logits)
        dest, tok_tab, blk_e, nxt_e, w_slot, n_used = _routing_tables(meta, counts, n_tok)
        ys = _moe_experts(l, blk_e, nxt_e, w_slot, n_used, tok_tab, h2, exp_w_gate, exp_b_gate, exp_w_up,
                          exp_b_up, exp_w_down, exp_b_down)
        xf = _moe_combine(dest, ys, xf, gates, gt2, final_g.reshape(1, d), seq, l == n_layers - 1)
    return xf.reshape(batch, seq, d)
```

```python
import functools
import math

import jax
import jax.numpy as jnp
from jax import lax
from jax.experimental import pallas as pl
from jax.experimental.pallas import tpu as pltpu

F32 = jnp.float32
BF16 = jnp.bfloat16
I32 = jnp.int32

D_MODEL = 2048
RWKV_HEAD_DIM = 64
RWKV_WIDTH = 1024
LORA_DECAY = 64
LORA_A = 64
LORA_VRES = 32
LORA_GATE = 160
RWKV_GN_EPS = 64e-5
RET_HEADS = 8
RET_HEAD_DIM = 128
RET_WIDTH = 1024
RET_GN_EPS = 1e-5
ROPE_BASE = 10000.0
N_EXPERTS = 32
TOP_K = 4
D_EXPERT = 512
SWIGLU_ALPHA = 1.702
SWIGLU_LIMIT = 7.0
NORM_EPS = 1e-6
N_ADA = 6
N_SHIFT = 3 * RWKV_WIDTH + LORA_DECAY + LORA_A + LORA_GATE
N_RET = 4 * RET_WIDTH

LANES = 128
SUBLANES = 8
MXU_DIM = 256

RET_OFF = 0
GATE_OFF = RET_OFF + N_RET
RW_OFF = GATE_OFF + 2 * D_MODEL
XG_OFF = RW_OFF + 3 * RWKV_WIDTH
XG_W = 256
XWA_OFF = XG_OFF + XG_W
XV_OFF = XWA_OFF + LANES
N_PROJ = XV_OFF + LANES

RWKV_CHUNK = 64
RWKV_ROWS = 256
RWKV_GROUPS_PER_BODY = 4
RET_ROWS = 256
MOE_ROWS = 256
MOE_WEIGHT_COPIES = 4
COMB_ROWS = 128
VMEM_LIMIT = 56 * 1024 * 1024


def _dot(a, b):
    return jnp.dot(a, b, preferred_element_type=F32)


def _dot_nt(a, b):
    return lax.dot_general(a, b, (((1,), (1,)), ((), ())), preferred_element_type=F32)


def _split_bf16(x):
    hi = x.astype(BF16)
    lo = (x - hi.astype(F32)).astype(BF16)
    return hi, lo


def _sigmoid(x):
    return 1.0 / (1.0 + jnp.exp(-x))


def _cparams(sem):
    return pltpu.CompilerParams(dimension_semantics=sem, vmem_limit_bytes=VMEM_LIMIT)


def _mod_kernel(c_ref, w_ref, b_ref, o_ref):
    c = c_ref[...]
    cs = c * _sigmoid(c)
    o_ref[0] = _dot(cs.astype(BF16), w_ref[0].astype(BF16)) + b_ref[0]


def _modulation(c_pad, ada_w, ada_b):
    n_layers, d, n = ada_w.shape
    tn = 1024
    return pl.pallas_call(
        _mod_kernel,
        grid=(n_layers, n // tn),
        in_specs=[
            pl.BlockSpec((SUBLANES, d), lambda l, j: (0, 0)),
            pl.BlockSpec((1, d, tn), lambda l, j: (l, 0, j)),
            pl.BlockSpec((1, 1, tn), lambda l, j: (l, 0, j)),
        ],
        out_specs=pl.BlockSpec((1, SUBLANES, tn), lambda l, j: (l, 0, j)),
        out_shape=jax.ShapeDtypeStruct((n_layers, SUBLANES, n), F32),
        compiler_params=_cparams(("parallel", "parallel")),
        name="adaln_mod",
    )(c_pad, ada_w, ada_b.reshape(n_layers, 1, n))


def _trig_kernel(pos_ref, invf_ref, cos_ref, sin_ref):
    ang = pos_ref[...].astype(F32) * invf_ref[...]
    lane = lax.broadcasted_iota(I32, ang.shape, 1)
    s = jnp.sin(ang)
    cos_ref[...] = jnp.cos(ang)
    sin_ref[...] = jnp.where(lane < RET_HEAD_DIM // 2, -s, s)


def _rotary_tables(pos_col, invf2):
    t = pos_col.shape[0]
    tm = 1024
    return pl.pallas_call(
        _trig_kernel,
        grid=(t // tm,),
        in_specs=[pl.BlockSpec((tm, 1), lambda i: (i, 0)), pl.BlockSpec((1, LANES), lambda i: (0, 0))],
        out_specs=[pl.BlockSpec((tm, LANES), lambda i: (i, 0))] * 2,
        out_shape=[jax.ShapeDtypeStruct((t, LANES), F32)] * 2,
        compiler_params=_cparams(("parallel",)),
        name="rotary_tables",
    )(pos_col, invf2)


def _in_kernel(x_ref, g_ref, sc_ref, sh_ref, w_ref, o_ref, h_scr):
    @pl.when(pl.program_id(1) == 0)
    def _():
        x = x_ref[...]
        y = x * lax.rsqrt(jnp.mean(x * x, axis=-1, keepdims=True) + NORM_EPS)
        h_scr[...] = (y * g_ref[...] * (1.0 + sc_ref[0]) + sh_ref[0]).astype(BF16)

    o_ref[...] = _dot(h_scr[...], w_ref[...])


def _in_proj(x, g, sc, sh, w, seq):
    t, d = x.shape
    n = w.shape[1]
    tm = min(1024, seq)
    tn = 512
    return pl.pallas_call(
        _in_kernel,
        grid=(t // tm, n // tn),
        in_specs=[
            pl.BlockSpec((tm, d), lambda i, j: (i, 0)),
            pl.BlockSpec((1, d), lambda i, j: (0, 0)),
            pl.BlockSpec((1, 1, d), lambda i, j: (i * tm // seq, 0, 0)),
            pl.BlockSpec((1, 1, d), lambda i, j: (i * tm // seq, 0, 0)),
            pl.BlockSpec((d, tn), lambda i, j: (0, j)),
        ],
        out_specs=pl.BlockSpec((tm, tn), lambda i, j: (i, j)),
        out_shape=jax.ShapeDtypeStruct((t, n), F32),
        scratch_shapes=[pltpu.VMEM((tm, d), BF16)],
        compiler_params=_cparams(("parallel", "arbitrary")),
        name="norm_in_proj",
    )(x, g, sc, sh, w)


def _bdot(a, b):
    return jnp.einsum("bij,bjk->bik", a, b, preferred_element_type=F32)


def _bdot_nt(a, b):
    return jnp.einsum("bik,bjk->bij", a, b, preferred_element_type=F32)


def _rwkv_chunk_terms(at, rt, bt, kt, bh, kh, v, wc_row):
    c = RWKV_CHUNK
    n = RWKV_HEAD_DIM
    lw = LANES
    nb = at.shape[0]
    reps = lw // c
    row = lax.broadcasted_iota(I32, (nb, lw, lw), 1)
    col = lax.broadcasted_iota(I32, (nb, lw, lw), 2)
    same_head = (row // n) == (col // n)

    def stack(x):
        return jnp.where(same_head, jnp.concatenate([x] * reps, axis=1), 0.0)

    def transpose(x):
        return jnp.stack([x[i].T for i in range(nb)])

    at2 = stack(at)
    rt2 = stack(rt)
    v2 = stack(v).astype(BF16)
    lhs = jnp.concatenate([at2, rt2], axis=1).astype(BF16)
    rhs = jnp.concatenate([stack(bt), stack(kt)], axis=1).astype(BF16)
    aa = _bdot_nt(lhs, rhs)
    strict = row > col
    lower = row >= col
    a_ab = jnp.where(strict, aa[:, :lw, :lw], 0.0)
    a_ak = jnp.where(strict, aa[:, :lw, lw:], 0.0)
    a_rb = jnp.where(lower, aa[:, lw:, :lw], 0.0)
    a_rk = jnp.where(lower, aa[:, lw:, lw:], 0.0)

    x = jnp.where(row == col, 1.0, 0.0) + jnp.where(((row ^ col) == 1) & ((row & 1) == 1), a_ab, 0.0)
    b = 2
    while b < c:
        sel = ((row // (2 * b)) == (col // (2 * b))) & ((row & b) != 0) & ((col & b) == 0)
        xb = x.astype(BF16)
        y = _bdot(xb, jnp.where(sel, a_ab, 0.0).astype(BF16))
        x = x + _bdot(y.astype(BF16), xb)
        b *= 2

    akv = _bdot(a_ak.astype(BF16), v2)
    p12b = _bdot(x.astype(BF16), jnp.concatenate([at2, akv], axis=2).astype(BF16)).astype(BF16)
    arb = _bdot(a_rb.astype(BF16), p12b)
    qp = (rt2 + arb[:, :, :lw]).astype(BF16)
    o_in = arb[:, :, lw:] + _bdot(a_rk.astype(BF16), v2)
    bp = _bdot(transpose(stack(bh)).astype(BF16), p12b)
    ht = bp[:, :, lw:] + _bdot(transpose(stack(kh)).astype(BF16), v2)
    wc_col = jnp.sum(jnp.where(row == col, wc_row, 0.0), axis=2, keepdims=True)
    return qp, o_in, bp[:, :, :lw].astype(BF16), ht, wc_col


def _rwkv_kernel(*refs, rows, has_vres):
    c = RWKV_CHUNK
    n = RWKV_HEAD_DIM
    lw = LANES
    ng = RWKV_WIDTH // lw
    nc = rows // c
    gpb = RWKV_GROUPS_PER_BODY
    it = iter(refs)
    r_ref, k_ref, v_ref, xg_ref, xwa_ref, xv_ref = (next(it) for _ in range(6))
    mu_r, mu_k, mu_v, mu_xg, mu_xwa, mu_xv = (next(it) for _ in range(6))
    w0, a0, kks, ka, rk, lng, lnb, v0 = (next(it) for _ in range(8))
    w2_ref, a2_ref, g2_ref, v2_ref = (next(it) for _ in range(4))
    vf_ref = next(it) if has_vres else None
    ya_ref = next(it)
    vfo_ref = None if has_vres else next(it)
    c_r, c_k, c_v, c_xg, c_xwa, c_xv = (next(it) for _ in range(6))
    st_s = next(it)
    at_s, rt_s, bt_s, kt_s, bh_s, kh_s, vv_s, wc_s, o_s = (next(it) for _ in range(9))
    bonus_s, gate_s = next(it), next(it)

    @pl.when(pl.program_id(1) == 0)
    def _():
        for cr in (c_r, c_k, c_v, c_xg, c_xwa, c_xv):
            cr[...] = jnp.zeros_like(cr)
        st_s[...] = jnp.zeros_like(st_s)

    def shift(p_ref, mu_ref, carry_ref):
        p = p_ref[...]
        row = lax.broadcasted_iota(I32, p.shape, 0)
        prev = jnp.where(row == 0, carry_ref[0:1, :], pltpu.roll(p, 1, 0))
        carry_ref[0:1, :] = p[rows - 1:rows, :]
        return p + (prev - p) * mu_ref[...]

    r = shift(r_ref, mu_r, c_r)
    k = shift(k_ref, mu_k, c_k)
    v = shift(v_ref, mu_v, c_v)
    xg = shift(xg_ref, mu_xg, c_xg)
    xwa = shift(xwa_ref, mu_xwa, c_xwa)

    z = w0[...] + _dot(jnp.tanh(xwa).astype(BF16), w2_ref[...])
    lw_log = -math.exp(-0.5) * _sigmoid(z)
    a = _sigmoid(a0[...] + _dot(xwa.astype(BF16), a2_ref[...]))
    gate_s[...] = _dot(_sigmoid(xg).astype(BF16), g2_ref[...])
    if has_vres:
        xv = shift(xv_ref, mu_xv, c_xv)
        v = v + (vf_ref[...] - v) * _sigmoid(v0[...] + _dot(xv.astype(BF16), v2_ref[...]))
    else:
        vfo_ref[...] = v

    hrow = lax.broadcasted_iota(I32, (MXU_DIM, MXU_DIM), 0) // n
    hcol = lax.broadcasted_iota(I32, (MXU_DIM, MXU_DIM), 1) // n
    head_ones = (hrow == hcol).astype(BF16)

    def headsum(x):
        xb = x.astype(BF16)
        return jnp.concatenate([_dot(xb[:, i * MXU_DIM:(i + 1) * MXU_DIM], head_ones)
                                for i in range(RWKV_WIDTH // MXU_DIM)], axis=1)

    kk0 = k * kks[...]
    kk = kk0 * lax.rsqrt(jnp.maximum(headsum(kk0 * kk0), 1e-24))
    km = k * (1.0 + (a - 1.0) * ka[...])
    bonus_s[...] = headsum(r * km * rk[...]) * v

    trow = lax.broadcasted_iota(I32, (rows, rows), 0) // c
    tcol = lax.broadcasted_iota(I32, (rows, rows), 1) // c
    same_chunk = trow == tcol
    tri = (same_chunk & (lax.broadcasted_iota(I32, (rows, rows), 1)
                         <= lax.broadcasted_iota(I32, (rows, rows), 0))).astype(BF16)
    blk_ones = same_chunk.astype(BF16)
    lhi, llo = _split_bf16(lw_log)
    cs = _dot(tri, lhi) + _dot(tri, llo)
    ctot = _dot(blk_ones, lhi) + _dot(blk_ones, llo)
    b_vec = kk * a
    w_inv = jnp.exp(-cs)
    tail = jnp.exp(ctot - cs)
    planes = ((at_s, -kk * jnp.exp(cs - lw_log)), (rt_s, r * jnp.exp(cs)), (bt_s, b_vec * w_inv),
              (kt_s, km * w_inv), (bh_s, b_vec * tail), (kh_s, km * tail), (vv_s, v), (wc_s, jnp.exp(ctot)))
    for dst, val in planes:
        for g in range(ng):
            dst[g] = val[:, g * lw:(g + 1) * lw]

    def pair_body(gp, carry):
        gs = pl.ds(gp * gpb, gpb)

        def chunks(ref):
            return ref[gs].reshape(gpb * nc, c, lw)

        qp_b, o_in_b, bpg_b, ht_b, wc_col_b = _rwkv_chunk_terms(
            chunks(at_s), chunks(rt_s), chunks(bt_s), chunks(kt_s), chunks(bh_s), chunks(kh_s),
            chunks(vv_s), chunks(wc_s)[:, 0:1, :])
        states = [st_s[gp * gpb + gg] for gg in range(gpb)]
        for ci in range(nc):
            for gg in range(gpb):
                bi = gg * nc + ci
                qp, o_in, bpg, ht, wc_col = qp_b[bi], o_in_b[bi], bpg_b[bi], ht_b[bi], wc_col_b[bi]
                st = states[gg]
                stb = st.astype(BF16)
                o2 = _dot(qp, stb) + o_in
                o = o2[0:c, :]
                for h in range(1, lw // c):
                    o = o + o2[h * c:(h + 1) * c, :]
                o_s[gp * gpb + gg, pl.ds(ci * c, c), :] = o
                states[gg] = wc_col * st + _dot(bpg, stb) + ht
        for gg in range(gpb):
            st_s[gp * gpb + gg] = states[gg]
        return carry

    lax.fori_loop(0, ng // gpb, pair_body, 0)

    o = jnp.concatenate([o_s[g] for g in range(ng)], axis=1)
    mean = headsum(o) * (1.0 / n)
    dev = o - mean
    var = headsum(dev * dev) * (1.0 / n)
    on = dev * lax.rsqrt(var + RWKV_GN_EPS) * lng[...] + lnb[...]
    ya_ref[...] = ((on + bonus_s[...]) * gate_s[...]).astype(ya_ref.dtype)


def _rwkv_mix(proj, mu_ext, pw, v_first, batch, seq):
    t = proj.shape[0]
    rows = min(RWKV_ROWS, seq)
    w = RWKV_WIDTH
    nj = seq // rows
    ng = w // LANES
    has_vres = v_first is not None

    def colspec(width, off):
        return pl.BlockSpec((rows, width), lambda b, j: (b * nj + j, off // width))

    def muspec(width, off):
        return pl.BlockSpec((1, width), lambda b, j: (0, off // width))

    layout = [(w, RW_OFF), (w, RW_OFF + w), (w, RW_OFF + 2 * w), (XG_W, XG_OFF), (LANES, XWA_OFF), (LANES, XV_OFF)]
    in_specs = [colspec(*a) for a in layout] + [muspec(*a) for a in layout]
    args = [proj] * 6 + [mu_ext] * 6
    for name in ("w0", "a0", "kks", "ka", "rk", "lng", "lnb", "v0"):
        in_specs.append(pl.BlockSpec((1, w), lambda b, j: (0, 0)))
        args.append(pw[name])
    for name, kdim in (("w2", LANES), ("a2", LANES), ("g2", XG_W), ("v2", LANES)):
        in_specs.append(pl.BlockSpec((kdim, w), lambda b, j: (0, 0)))
        args.append(pw[name])
    io_spec = pl.BlockSpec((rows, w), lambda b, j: (b * nj + j, 0))
    out_specs = [io_spec]
    out_shape = [jax.ShapeDtypeStruct((t, w), BF16)]
    if has_vres:
        in_specs.append(io_spec)
        args.append(v_first)
    else:
        out_specs.append(io_spec)
        out_shape.append(jax.ShapeDtypeStruct((t, w), F32))
    scratch = [pltpu.VMEM((SUBLANES, w), F32)] * 3 + [pltpu.VMEM((SUBLANES, XG_W), F32)]
    scratch += [pltpu.VMEM((SUBLANES, LANES), F32)] * 2
    scratch += [pltpu.VMEM((ng, LANES, LANES), F32)]
    scratch += [pltpu.VMEM((ng, rows, LANES), F32)] * 9
    scratch += [pltpu.VMEM((rows, w), F32)] * 2
    outs = pl.pallas_call(
        functools.partial(_rwkv_kernel, rows=rows, has_vres=has_vres),
        grid=(batch, nj),
        in_specs=in_specs,
        out_specs=out_specs,
        out_shape=out_shape,
        scratch_shapes=scratch,
        compiler_params=_cparams(("parallel", "arbitrary")),
        name="rwkv7_mix",
    )(*args)
    if has_vres:
        return outs[0], v_first
    return outs[0], outs[1]


def _ret_kernel(q_ref, k_ref, v_ref, g_ref, cos_ref, sin_ref, lg_ref, gn_ref, o_ref, st_ref, *, rows):
    @pl.when(pl.program_id(1) == 0)
    def _():
        st_ref[...] = jnp.zeros_like(st_ref)

    hd = RET_HEAD_DIM
    cos = cos_ref[...]
    sin = sin_ref[...]
    row = lax.broadcasted_iota(I32, (rows, rows), 0)
    col = lax.broadcasted_iota(I32, (rows, rows), 1)
    causal = row >= col
    rel = jnp.maximum(row - col, 0).astype(F32)
    idx = lax.broadcasted_iota(I32, (rows, LANES), 0).astype(F32)

    def rot(x):
        return x * cos + pltpu.roll(x, hd // 2, 1) * sin

    for h in range(RET_HEADS):
        hs = slice(h * hd, (h + 1) * hd)
        lg = lg_ref[h]
        q = rot(q_ref[:, hs])
        k = rot(k_ref[:, hs]) * (hd ** -0.5)
        v = v_ref[:, hs].astype(BF16)
        lg_sq = jnp.concatenate([lg] * (rows // LANES), axis=1) if rows > LANES else lg
        decay = jnp.where(causal, jnp.exp(rel * lg_sq), 0.0)
        qb = q.astype(BF16)
        scores = _dot_nt(qb, k.astype(BF16)) * decay
        intra = _dot(scores.astype(BF16), v)
        xi = jnp.exp((idx + 1.0) * lg)
        st = st_ref[h]
        cross = _dot(qb, st.astype(BF16)) * xi
        zeta = jnp.exp((rows - 1.0 - idx) * lg)
        kz = (k * zeta).T.astype(BF16)
        chunk_decay = jnp.exp(rows * lg)
        st_ref[h] = st * chunk_decay + _dot(kz, v)
        o = intra + cross
        mean = jnp.mean(o, axis=-1, keepdims=True)
        dev = o - mean
        var = jnp.mean(dev * dev, axis=-1, keepdims=True)
        on = dev * lax.rsqrt(var + RET_GN_EPS) * gn_ref[:, hs]
        gr = g_ref[:, hs]
        o_ref[:, hs] = (gr * _sigmoid(gr) * on).astype(o_ref.dtype)


def _retention(proj, cos2, sin2, log_gamma, gn_g, batch, seq):
    t = proj.shape[0]
    rows = min(RET_ROWS, seq)
    nj = seq // rows
    w = RET_WIDTH

    def pspec(off):
        return pl.BlockSpec((rows, w), lambda b, j: (b * nj + j, off // w))

    tspec = pl.BlockSpec((rows, LANES), lambda b, j: (b * nj + j, 0))
    return pl.pallas_call(
        functools.partial(_ret_kernel, rows=rows),
        grid=(batch, nj),
        in_specs=[pspec(RET_OFF), pspec(RET_OFF + w), pspec(RET_OFF + 2 * w), pspec(RET_OFF + 3 * w),
                  tspec, tspec,
                  pl.BlockSpec((RET_HEADS, 1, LANES), lambda b, j: (0, 0, 0)),
                  pl.BlockSpec((1, w), lambda b, j: (0, 0))],
        out_specs=pl.BlockSpec((rows, w), lambda b, j: (b * nj + j, 0)),
        out_shape=jax.ShapeDtypeStruct((t, w), BF16),
        scratch_shapes=[pltpu.VMEM((RET_HEADS, RET_HEAD_DIM, RET_HEAD_DIM), F32)],
        compiler_params=_cparams(("parallel", "arbitrary")),
        name="retention",
    )(proj, proj, proj, proj, cos2, sin2, log_gamma, gn_g)


def _out_kernel(ya_ref, yb_ref, ga_ref, gb_ref, x_ref, gt1_ref, n2g_ref, sc2_ref, sh2_ref,
                wa_ref, wb_ref, wo_ref, rwh_ref, rwl_ref, rb_ref, xo_ref, h2_ref, lg_ref):
    ua = _dot(ya_ref[...], wa_ref[...])
    ub = _dot(yb_ref[...], wb_ref[...])
    y = _sigmoid(ga_ref[...]) * ua + _sigmoid(gb_ref[...]) * ub
    xn = x_ref[...] + gt1_ref[0] * _dot(y.astype(BF16), wo_ref[...])
    xo_ref[...] = xn
    yn = xn * lax.rsqrt(jnp.mean(xn * xn, axis=-1, keepdims=True) + NORM_EPS)
    h2 = yn * n2g_ref[...] * (1.0 + sc2_ref[0]) + sh2_ref[0]
    h2_ref[...] = h2
    hi, lo = _split_bf16(h2)
    lg_ref[...] = _dot(hi, rwh_ref[...]) + _dot(hi, rwl_ref[...]) + _dot(lo, rwh_ref[...]) + rb_ref[...]


def _merge_out(ya, yb, proj, x, gt1, n2g, sc2, sh2, wa, wb, wo, rwh, rwl, rb, seq):
    t, d = x.shape
    tm = 256
    nb = seq // tm
    const = lambda i: (0, 0)
    mod = pl.BlockSpec((1, 1, d), lambda i: (i // nb, 0, 0))
    single = pl.Buffered(1)
    return pl.pallas_call(
        _out_kernel,
        grid=(t // tm,),
        in_specs=[
            pl.BlockSpec((tm, RWKV_WIDTH), lambda i: (i, 0)),
            pl.BlockSpec((tm, RET_WIDTH), lambda i: (i, 0)),
            pl.BlockSpec((tm, d), lambda i: (i, GATE_OFF // d)),
            pl.BlockSpec((tm, d), lambda i: (i, GATE_OFF // d + 1)),
            pl.BlockSpec((tm, d), lambda i: (i, 0)),
            mod, pl.BlockSpec((1, d), const), mod, mod,
            pl.BlockSpec((RWKV_WIDTH, d), const, pipeline_mode=single),
            pl.BlockSpec((RET_WIDTH, d), const, pipeline_mode=single),
            pl.BlockSpec((d, d), const, pipeline_mode=single),
            pl.BlockSpec((d, LANES), const, pipeline_mode=single),
            pl.BlockSpec((d, LANES), const, pipeline_mode=single),
            pl.BlockSpec((1, LANES), const),
        ],
        out_specs=[pl.BlockSpec((tm, d), lambda i: (i, 0)), pl.BlockSpec((tm, d), lambda i: (i, 0)),
                   pl.BlockSpec((tm, LANES), lambda i: (i, 0))],
        out_shape=[jax.ShapeDtypeStruct((t, d), F32), jax.ShapeDtypeStruct((t, d), F32),
                   jax.ShapeDtypeStruct((t, LANES), F32)],
        compiler_params=_cparams(("parallel",)),
        name="merge_out_norm_router",
    )(ya, yb, proj, proj, x, gt1, n2g, sc2, sh2, wa, wb, wo, rwh, rwl, rb)


def _route_kernel(lg_ref, meta_ref, gate_ref, cnt_ref, carry_ref, *, tm):
    @pl.when(pl.program_id(0) == 0)
    def _():
        carry_ref[...] = jnp.zeros_like(carry_ref)

    l = lg_ref[...]
    lane = lax.broadcasted_iota(I32, l.shape, 1)
    lane_f = lane.astype(F32)
    vals, idxs = [], []
    for _ in range(TOP_K):
        m = jnp.max(l, axis=-1, keepdims=True)
        idx = jnp.min(jnp.where(l == m, lane_f, float(LANES)), axis=-1, keepdims=True)
        vals.append(m)
        idxs.append(idx)
        l = jnp.where(lane_f == idx, -jnp.inf, l)
    es = [jnp.exp(vv - vals[0]) for vv in vals]
    denom = es[0] + es[1] + es[2] + es[3]
    sel = jnp.zeros(l.shape, F32)
    for idx in idxs:
        sel = sel + jnp.where(lane_f == idx, 1.0, 0.0)
    row = lax.broadcasted_iota(I32, (tm, tm), 0)
    col = lax.broadcasted_iota(I32, (tm, tm), 1)
    tri = (col <= row).astype(BF16)
    cum = _dot(tri, sel.astype(BF16)) + carry_ref[0:1, :]
    excl = cum - sel
    carry_ref[0:1, :] = cum[tm - 1:tm, :]
    meta = jnp.zeros(l.shape, F32)
    gates = jnp.zeros(l.shape, F32)
    for kk in range(TOP_K):
        rank = jnp.sum(jnp.where(lane_f == idxs[kk], excl, 0.0), axis=-1, keepdims=True)
        meta = meta + jnp.where(lane == kk, idxs[kk], 0.0) + jnp.where(lane == TOP_K + kk, rank, 0.0)
        gates = gates + jnp.where(lane == kk, es[kk] / denom, 0.0)
    meta_ref[...] = meta.astype(I32)
    gate_ref[...] = gates
    cnt_ref[...] = jnp.broadcast_to(cum[tm - 1:tm, :], cnt_ref.shape).astype(I32)


def _route(logits):
    t = logits.shape[0]
    tm = 256
    return pl.pallas_call(
        functools.partial(_route_kernel, tm=tm),
        grid=(t // tm,),
        in_specs=[pl.BlockSpec((tm, LANES), lambda i: (i, 0))],
        out_specs=[pl.BlockSpec((tm, LANES), lambda i: (i, 0)), pl.BlockSpec((tm, LANES), lambda i: (i, 0)),
                   pl.BlockSpec((SUBLANES, LANES), lambda i: (0, 0))],
        out_shape=[jax.ShapeDtypeStruct((t, LANES), I32), jax.ShapeDtypeStruct((t, LANES), F32),
                   jax.ShapeDtypeStruct((SUBLANES, LANES), I32)],
        scratch_shapes=[pltpu.VMEM((SUBLANES, LANES), F32)],
        compiler_params=_cparams(("arbitrary",)),
        name="router_topk",
    )(logits)


def _moe_kernel(be_ref, nx_ref, ws_ref, nu_ref, tok_ref, h_hbm, wg_hbm, wu_hbm, wd_hbm, bg_ref, bu_ref, bd_ref,
                ys_ref, xbuf, xb_s, wg_buf, wu_buf, wd_buf, wgu_s, wd_s, sem, wsem, *, layer):
    i = pl.program_id(0)
    n_used = nu_ref[0]
    blk = MOE_ROWS
    f = D_EXPERT
    e = be_ref[i]
    wslot = ws_ref[i]

    def row_copy(tok, r, slot):
        return pltpu.make_async_copy(h_hbm.at[pl.ds(tok, 1), :], xbuf.at[slot, pl.ds(r, 1), :], sem.at[slot])

    def wait_block(slot):
        pltpu.make_async_copy(h_hbm.at[pl.ds(0, blk), :], xbuf.at[slot], sem.at[slot]).wait()

    def weight_copies(expert, slot):
        copies = []
        for src, dst in ((wg_hbm, wg_buf), (wu_hbm, wu_buf), (wd_hbm, wd_buf)):
            rows = src.shape[2] // MOE_WEIGHT_COPIES
            for part in range(MOE_WEIGHT_COPIES):
                rs = pl.ds(part * rows, rows)
                copies.append(pltpu.make_async_copy(src.at[layer, expert, rs, :], dst.at[slot, rs, :],
                                                    wsem.at[slot]))
        return copies

    @pl.when(i == 0)
    def _():
        for cp in weight_copies(e, wslot):
            cp.start()

        def body(r, carry):
            row_copy(tok_ref[r], r, 0).start()
            return carry
        lax.fori_loop(0, blk, body, 0)

    prev_e = be_ref[jnp.maximum(i - 1, 0)]

    @pl.when((i < n_used) & ((i == 0) | (e != prev_e)))
    def _():
        for cp in weight_copies(e, wslot):
            cp.wait()
        wgu_s[:, :f] = wg_buf[wslot].astype(BF16)
        wgu_s[:, f:] = wu_buf[wslot].astype(BF16)
        wd_s[...] = wd_buf[wslot].astype(BF16)
        nxt = nx_ref[i]

        @pl.when(nxt >= 0)
        def _():
            for cp in weight_copies(nxt, 1 - wslot):
                cp.start(priority=1)

    @pl.when(i < n_used)
    def _():
        slot = i % 2
        wait_block(slot)
        xb_s[...] = xbuf[slot].astype(BF16)
        base = (i + 1) * blk
        for r in range(blk):
            row_copy(tok_ref[base + r], r, 1 - slot).start()
        x = xb_s[...]
        gt = jnp.minimum(_dot(x, wgu_s[:, :f]) + bg_ref[0, 0], SWIGLU_LIMIT)
        up = jnp.clip(_dot(x, wgu_s[:, f:]) + bu_ref[0, 0], -SWIGLU_LIMIT, SWIGLU_LIMIT)
        act = (up + 1.0) * (gt * _sigmoid(gt * SWIGLU_ALPHA))
        ys_ref[...] = _dot(act.astype(BF16), wd_s[...]) + bd_ref[0, 0]

    @pl.when(i == n_used)
    def _():
        wait_block(i % 2)

    @pl.when(i >= n_used)
    def _():
        ys_ref[...] = jnp.zeros_like(ys_ref)


def _moe_experts(layer, blk_e, nxt_e, w_slot, n_used, tok_tab, h2, w_gate, b_gate, w_up, b_up, w_down, b_down):
    d = D_MODEL
    f = D_EXPERT
    n_pad = tok_tab.shape[0]
    blk = MOE_ROWS
    nblk = n_pad // blk
    bmap = lambda i, be, nx, ws, nu, tk: (layer, be[i], 0, 0)
    grid_spec = pltpu.PrefetchScalarGridSpec(
        num_scalar_prefetch=5,
        grid=(nblk,),
        in_specs=[
            pl.BlockSpec(memory_space=pl.ANY),
            pl.BlockSpec(memory_space=pl.ANY),
            pl.BlockSpec(memory_space=pl.ANY),
            pl.BlockSpec(memory_space=pl.ANY),
            pl.BlockSpec((1, 1, 1, f), bmap),
            pl.BlockSpec((1, 1, 1, f), bmap),
            pl.BlockSpec((1, 1, 1, d), bmap),
        ],
        out_specs=pl.BlockSpec((blk, d), lambda i, be, nx, ws, nu, tk: (i, 0)),
        scratch_shapes=[pltpu.VMEM((2, blk, d), F32), pltpu.VMEM((blk, d), BF16),
                        pltpu.VMEM((2, d, f), F32), pltpu.VMEM((2, d, f), F32), pltpu.VMEM((2, f, d), F32),
                        pltpu.VMEM((d, 2 * f), BF16), pltpu.VMEM((f, d), BF16),
                        pltpu.SemaphoreType.DMA((2,)), pltpu.SemaphoreType.DMA((2,))],
    )
    n_layers, n_exp = w_gate.shape[:2]
    return pl.pallas_call(
        functools.partial(_moe_kernel, layer=layer),
        grid_spec=grid_spec,
        out_shape=jax.ShapeDtypeStruct((n_pad, d), F32),
        compiler_params=_cparams(("arbitrary",)),
        name="moe_experts",
    )(blk_e, nxt_e, w_slot, n_used, tok_tab, h2, w_gate, w_up, w_down, b_gate.reshape(n_layers, n_exp, 1, f),
      b_up.reshape(n_layers, n_exp, 1, f), b_down.reshape(n_layers, n_exp, 1, d))


def _comb_kernel(dest_ref, ys_hbm, x_ref, gate_ref, gt2_ref, fg_ref, o_ref, buf, sem, *, final, n_tiles):
    i = pl.program_id(0)
    tc = COMB_ROWS

    def row_copy(base, r, kk, slot):
        return pltpu.make_async_copy(ys_hbm.at[pl.ds(dest_ref[base + r * TOP_K + kk], 1), :],
                                     buf.at[slot, kk, pl.ds(r, 1), :], sem.at[slot])

    def wait_tile(slot):
        for kk in range(TOP_K):
            pltpu.make_async_copy(ys_hbm.at[pl.ds(0, tc), :], buf.at[slot, kk], sem.at[slot]).wait()

    @pl.when(i == 0)
    def _():
        def body(r, carry):
            for kk in range(TOP_K):
                row_copy(0, r, kk, 0).start()
            return carry
        lax.fori_loop(0, tc, body, 0)

    slot = i % 2
    wait_tile(slot)
    base = jnp.minimum(i + 1, n_tiles - 1) * (tc * TOP_K)
    for r in range(tc):
        for kk in range(TOP_K):
            row_copy(base, r, kk, 1 - slot).start(priority=kk % 2)
    gates = gate_ref[...]
    moe = gates[:, 0:1] * buf[slot, 0]
    for kk in range(1, TOP_K):
        moe = moe + gates[:, kk:kk + 1] * buf[slot, kk]
    xn = x_ref[...] + gt2_ref[0] * moe
    if final:
        xn = xn * lax.rsqrt(jnp.mean(xn * xn, axis=-1, keepdims=True) + NORM_EPS) * fg_ref[...]
    o_ref[...] = xn

    @pl.when(i == n_tiles - 1)
    def _():
        wait_tile(1 - slot)


def _moe_combine(dest_flat, ys, x, gates, gt2, final_g, seq, final):
    t, d = x.shape
    tc = COMB_ROWS
    nb = seq // tc
    n_tiles = t // tc
    grid_spec = pltpu.PrefetchScalarGridSpec(
        num_scalar_prefetch=1,
        grid=(n_tiles,),
        in_specs=[
            pl.BlockSpec(memory_space=pl.ANY),
            pl.BlockSpec((tc, d), lambda i, ds: (i, 0)),
            pl.BlockSpec((tc, LANES), lambda i, ds: (i, 0)),
            pl.BlockSpec((1, 1, d), lambda i, ds: (i // nb, 0, 0)),
            pl.BlockSpec((1, d), lambda i, ds: (0, 0)),
        ],
        out_specs=pl.BlockSpec((tc, d), lambda i, ds: (i, 0)),
        scratch_shapes=[pltpu.VMEM((2, TOP_K, tc, d), F32), pltpu.SemaphoreType.DMA((2,))],
    )
    return pl.pallas_call(
        functools.partial(_comb_kernel, final=final, n_tiles=n_tiles),
        grid_spec=grid_spec,
        out_shape=jax.ShapeDtypeStruct((t, d), F32),
        compiler_params=_cparams(("arbitrary",)),
        name="moe_combine",
    )(dest_flat, ys, x, gates, gt2, final_g)


def _pad_rows(w, n_rows, at=0):
    out = jnp.zeros((n_rows, w.shape[1]), w.dtype)
    return out.at[at:at + w.shape[0]].set(w)


def _layer_params(l, p):
    d = D_MODEL
    w_in = p["w_in"][l]
    sh, ret, gate = w_in[:, :N_SHIFT], w_in[:, N_SHIFT:N_SHIFT + N_RET], w_in[:, N_SHIFT + N_RET:]
    rkv = sh[:, :3 * RWKV_WIDTH]
    xw = sh[:, 3 * RWKV_WIDTH:3 * RWKV_WIDTH + LORA_DECAY]
    xa = sh[:, 3 * RWKV_WIDTH + LORA_DECAY:3 * RWKV_WIDTH + LORA_DECAY + LORA_A]
    xg = sh[:, 3 * RWKV_WIDTH + LORA_DECAY + LORA_A:]
    xg = jnp.pad(xg, ((0, 0), (0, XG_W - LORA_GATE)))
    mu = p["shift_mu"][l]
    mu_xg = jnp.pad(mu[3 * RWKV_WIDTH + LORA_DECAY + LORA_A:], (0, XG_W - LORA_GATE))
    mu_xv = jnp.concatenate([jnp.zeros((LORA_VRES,), F32), jnp.ones((LORA_VRES,), F32),
                             jnp.zeros((LANES - 2 * LORA_VRES,), F32)])
    if l == 0:
        xv = jnp.zeros((d, LANES), F32)
        v2 = jnp.zeros((LANES, RWKV_WIDTH), F32)
        v0 = jnp.zeros((RWKV_WIDTH,), F32)
    else:
        mu_v = p["vres_mu"][l - 1][:, None]
        v1 = p["vres_w1"][l - 1]
        xv = jnp.concatenate([v1 * (1.0 - mu_v), v1 * mu_v, jnp.zeros((d, LANES - 2 * LORA_VRES), F32)], axis=1)
        w2v = p["vres_w2"][l - 1]
        v2 = _pad_rows(jnp.concatenate([w2v, w2v], axis=0), LANES)
        v0 = p["vres_v0"][l - 1]
    w_proj = jnp.concatenate([ret, gate, rkv, xg, xw, xa, xv], axis=1).astype(BF16)
    mu_ext = jnp.concatenate([jnp.zeros((N_RET + 2 * d,), F32), mu[:3 * RWKV_WIDTH], mu_xg,
                              mu[3 * RWKV_WIDTH:3 * RWKV_WIDTH + LORA_DECAY + LORA_A], mu_xv])[None, :]
    row = lambda a: a.reshape(1, -1)
    pw = {
        "w0": row(p["rwkv_w0"][l]), "a0": row(p["rwkv_a0"][l]), "kks": row(p["rwkv_kk_scale"][l]),
        "ka": row(p["rwkv_k_a"][l]), "rk": row(p["rwkv_r_k"][l]), "lng": row(p["rwkv_lnx_g"][l]),
        "lnb": row(p["rwkv_lnx_b"][l]), "v0": row(v0),
        "w2": _pad_rows(p["rwkv_w2"][l], LANES, 0).astype(BF16),
        "a2": _pad_rows(p["rwkv_a2"][l], LANES, LORA_DECAY).astype(BF16),
        "g2": _pad_rows(p["rwkv_g2"][l], XG_W, 0).astype(BF16),
        "v2": v2.astype(BF16),
    }
    rw = jnp.pad(p["router_w"][l], ((0, 0), (0, LANES - N_EXPERTS)))
    rwh, rwl = _split_bf16(rw)
    rb = jnp.concatenate([p["router_b"][l], jnp.full((LANES - N_EXPERTS,), -1e30, F32)])[None, :]
    return {
        "w_proj": w_proj, "mu_ext": mu_ext, "pw": pw,
        "wa": p["w_up_a"][l].astype(BF16), "wb": p["w_up_b"][l].astype(BF16), "wo": p["w_out"][l].astype(BF16),
        "rwh": rwh, "rwl": rwl, "rb": rb,
        "n1g": row(p["norm1_g"][l]), "n2g": row(p["norm2_g"][l]), "gn_g": row(p["ret_gn_g"][l]),
    }


def _routing_tables(meta, counts, n_tok):
    blk = MOE_ROWS
    idx = meta[:, :TOP_K]
    rank = meta[:, TOP_K:2 * TOP_K]
    cnt = counts[0, :N_EXPERTS]
    pcnt = (cnt + blk - 1) // blk * blk
    pends = jnp.cumsum(pcnt)
    pstarts = pends - pcnt
    dest = (pstarts[idx] + rank).astype(I32)
    n_pad = n_tok * TOP_K + N_EXPERTS * blk
    nblk = n_pad // blk
    tok = jnp.broadcast_to(jnp.arange(n_tok, dtype=I32)[:, None], (n_tok, TOP_K))
    tok_tab = jnp.zeros((n_pad,), I32).at[dest.reshape(-1)].set(tok.reshape(-1), unique_indices=True)
    starts = jnp.arange(nblk, dtype=I32) * blk
    blk_e = jnp.minimum(jnp.sum((pends[None, :] <= starts[:, None]).astype(I32), axis=1), N_EXPERTS - 1)
    n_used = (pends[-1] // blk).astype(I32).reshape(1)
    ids = jnp.arange(N_EXPERTS, dtype=I32)
    used = cnt > 0
    later_used = (ids[None, :] > ids[:, None]) & used[None, :]
    nxt = jnp.min(jnp.where(later_used, ids[None, :], N_EXPERTS), axis=1)
    nxt = jnp.where(nxt >= N_EXPERTS, -1, nxt).astype(I32)
    slot = ((jnp.cumsum(used.astype(I32)) - 1) % 2).astype(I32)
    return dest.reshape(-1), tok_tab, blk_e, nxt[blk_e], slot[blk_e], n_used


def kernel(x, c, positions, ada_w, ada_b, norm1_g, norm2_g, w_in, shift_mu, rwkv_w0, rwkv_w2, rwkv_a0, rwkv_a2, rwkv_g2, rwkv_kk_scale, rwkv_k_a, rwkv_r_k, rwkv_lnx_g, rwkv_lnx_b, vres_mu, vres_w1, vres_w2, vres_v0, ret_gn_g, w_up_a, w_up_b, w_out, router_w, router_b, exp_w_gate, exp_b_gate, exp_w_up, exp_b_up, exp_w_down, exp_b_down, final_g):
    p = dict(w_in=w_in, shift_mu=shift_mu, rwkv_w0=rwkv_w0, rwkv_w2=rwkv_w2, rwkv_a0=rwkv_a0,
             rwkv_a2=rwkv_a2, rwkv_g2=rwkv_g2, rwkv_kk_scale=rwkv_kk_scale, rwkv_k_a=rwkv_k_a,
             rwkv_r_k=rwkv_r_k, rwkv_lnx_g=rwkv_lnx_g, rwkv_lnx_b=rwkv_lnx_b, vres_mu=vres_mu,
             vres_w1=vres_w1, vres_w2=vres_w2, vres_v0=vres_v0, ret_gn_g=ret_gn_g, w_up_a=w_up_a,
             w_up_b=w_up_b, w_out=w_out, router_w=router_w, router_b=router_b, norm1_g=norm1_g,
             norm2_g=norm2_g)
    batch, seq, d = x.shape
    n_layers = ada_w.shape[0]
    n_tok = batch * seq
    assert batch <= SUBLANES and d == D_MODEL

    c_pad = jnp.zeros((SUBLANES, d), F32).at[:batch].set(c)
    mod = _modulation(c_pad, ada_w, ada_b)[:, :batch].reshape(n_layers, batch, N_ADA, 1, d)

    half = RET_HEAD_DIM // 2
    inv_freq = ROPE_BASE ** (-jnp.arange(half, dtype=F32) / half)
    invf2 = jnp.concatenate([inv_freq, inv_freq])[None, :]
    cos2, sin2 = _rotary_tables(positions.reshape(n_tok, 1), invf2)
    log_gamma = jnp.log1p(-jnp.exp2(-5.0 - jnp.arange(RET_HEADS, dtype=F32)))
    log_gamma = jnp.broadcast_to(log_gamma[:, None, None], (RET_HEADS, 1, LANES))

    xf = x.reshape(n_tok, d)
    v_first = None
    for l in range(n_layers):
        lp = _layer_params(l, p)
        sh1, sc1, gt1, sh2, sc2, gt2 = (mod[l, :, i] for i in range(N_ADA))
        proj = _in_proj(xf, lp["n1g"], sc1, sh1, lp["w_proj"], seq)
        ya, v_first = _rwkv_mix(proj, lp["mu_ext"], lp["pw"], v_first, batch, seq)
        yb = _retention(proj, cos2, sin2, log_gamma, lp["gn_g"], batch, seq)
        xf, h2, logits = _merge_out(ya, yb, proj, xf, gt1, lp["n2g"], sc2, sh2, lp["wa"], lp["wb"],
                                    lp["wo"], lp["rwh"], lp["rwl"], lp["rb"], seq)
        meta, gates, counts = _route(logits)
        dest, tok_tab, blk_e, nxt_e, w_slot, n_used = _routing_tables(meta, counts, n_tok)
        ys = _moe_experts(l, blk_e, nxt_e, w_slot, n_used, tok_tab, h2, exp_w_gate, exp_b_gate, exp_w_up,
                          exp_b_up, exp_w_down, exp_b_down)
        xf = _moe_combine(dest, ys, xf, gates, gt2, final_g.reshape(1, d), seq, l == n_layers - 1)
    return xf.reshape(batch, seq, d)
```
